```python
import functools
import jax, jax.numpy as jnp
from jax import lax
import numpy as np

D_MODEL = 1024
BATCH = 32
SEQ = 256
DEPTH = 1
DEC_BATCH = 8
DEC_SEQ = 1024
PAST_LEN = 256

GRID_W = 64
N_HEADS = 8
N_KV_HEADS = 2
HEAD_DIM = 64
GROUP = N_HEADS // N_KV_HEADS
ATTN_W = N_HEADS * HEAD_DIM
KV_W = N_KV_HEADS * HEAD_DIM
WINDOW = 128
ATTN_SCALE = HEAD_DIM ** -0.5
ROPE_BASE = 10000.0
ROPE_PAIRS_AXIS = HEAD_DIM // 4
D_RNN = 512
N_RNN_BLOCKS = 8
RNN_BLOCK = D_RNN // N_RNN_BLOCKS
CONV_W = 4
CONV_LEFT = 2
RG_C = 8.0
D_MIX = ATTN_W + D_RNN
D_IN = ATTN_W + 2 * KV_W + 2 * D_RNN
N_MOD = 6
PEER_HEADS = 8
N_KEYS = 128
N_EXPERTS = N_KEYS * N_KEYS
PEER_DQ = 256
PEER_TOPK = 16
PEER_BLOCK = 128
EPS = 1e-6

kernel_name = "hymba_rglru_swa_peer_diffusion_step"


def rms_norm(x, g):
    xf = x.astype(jnp.float32)
    y = xf * lax.rsqrt(jnp.mean(xf * xf, axis=-1, keepdims=True) + EPS)
    return (y * g.astype(jnp.float32)).astype(x.dtype)


def modulated_norm(x, g, shift, scale):
    return rms_norm(x, g) * (1 + scale) + shift


def adaln_mods(cvec, w_mod, b_mod):
    m = jnp.einsum('...d,de->...e', jax.nn.silu(cvec), w_mod) + b_mod
    return jnp.split(m, N_MOD, axis=-1)


def centred_dwconv(x, conv_w, conv_b):
    S = x.shape[1]
    xp = jnp.pad(x, ((0, 0), (CONV_LEFT, CONV_W - 1 - CONV_LEFT), (0, 0)))
    y = conv_b
    for j in range(CONV_W):
        y = y + xp[:, j:j + S] * conv_w[j]
    return y


def mixer_inputs(h, w_in, conv_w, conv_b):
    B, S = h.shape[:2]
    p = jnp.einsum('bsd,dp->bsp', h, w_in)
    q, k, v, xr, yg = jnp.split(p, [ATTN_W, ATTN_W + KV_W, ATTN_W + 2 * KV_W, ATTN_W + 2 * KV_W + D_RNN], axis=-1)
    q = q.reshape(B, S, N_HEADS, HEAD_DIM)
    k = k.reshape(B, S, N_KV_HEADS, HEAD_DIM)
    v = v.reshape(B, S, N_KV_HEADS, HEAD_DIM)
    xr = centred_dwconv(xr, conv_w, conv_b)
    return q, k, v, xr, yg


def block_diag(x, w):
    xb = x.reshape(x.shape[:-1] + (N_RNN_BLOCKS, RNN_BLOCK))
    return jnp.einsum('bsnc,ncd->bsnd', xb, w).reshape(x.shape)


def _lin_combine(left, right):
    a1, b1 = left
    a2, b2 = right
    return (a1 * a2, a2 * b1 + b2)


def rglru_direction(xr, w_a, b_a, w_i, b_i, lam, h0, reverse):
    xf = xr.astype(jnp.float32)
    r = jax.nn.sigmoid(block_diag(xf, w_a) + b_a)
    gi = jax.nn.sigmoid(block_diag(xf, w_i) + b_i)
    log_a = -RG_C * r * jax.nn.softplus(-lam.astype(jnp.float32))
    a = jnp.exp(log_a)
    b = jnp.sqrt(-jnp.expm1(2.0 * log_a)) * gi * xf
    a_cum, b_cum = lax.associative_scan(_lin_combine, (a, b), reverse=reverse, axis=1)
    return a_cum * h0[:, None, :].astype(jnp.float32) + b_cum


def rglru_mixer(xr, yg, rg_w_a, rg_b_a, rg_w_i, rg_b_i, rg_lambda, h0_f, h0_b):
    hf = rglru_direction(xr, rg_w_a[0], rg_b_a[0], rg_w_i[0], rg_b_i[0], rg_lambda[0], h0_f, False)
    hb = rglru_direction(xr, rg_w_a[1], rg_b_a[1], rg_w_i[1], rg_b_i[1], rg_lambda[1], h0_b, True)
    o = ((hf + hb) * jax.nn.gelu(yg.astype(jnp.float32))).astype(xr.dtype)
    return o, hf[:, -1], hb[:, 0]


def axial_rope(rows):
    row = jnp.repeat(jnp.arange(rows, dtype=jnp.float32), GRID_W)
    col = jnp.tile(jnp.arange(GRID_W, dtype=jnp.float32), rows)
    inv = ROPE_BASE ** (-jnp.arange(ROPE_PAIRS_AXIS, dtype=jnp.float32) / ROPE_PAIRS_AXIS)
    ang = jnp.concatenate([row[:, None] * inv, col[:, None] * inv], axis=-1)
    return jnp.cos(ang), jnp.sin(ang)


def apply_rope(x, cos, sin):
    xf = x.astype(jnp.float32)
    c = cos[None, :, None, :]
    s = sin[None, :, None, :]
    x1, x2 = xf[..., :HEAD_DIM // 2], xf[..., HEAD_DIM // 2:]
    return jnp.concatenate([x1 * c - x2 * s, x2 * c + x1 * s], axis=-1).astype(x.dtype)


def context_attention(q, k, v, sink):
    B, S = q.shape[:2]
    qg = q.reshape(B, S, N_KV_HEADS, GROUP, HEAD_DIM)
    s = jnp.einsum('bqkgd,bckd->bkgqc', qg, k).astype(jnp.float32) * ATTN_SCALE
    sk = jnp.broadcast_to(sink.astype(jnp.float32).reshape(1, N_KV_HEADS, GROUP, 1, 1), s.shape[:-1] + (1,))
    p = jax.nn.softmax(jnp.concatenate([s, sk], axis=-1), axis=-1)[..., :-1]
    o = jnp.einsum('bkgqc,bckd->bqkgd', p.astype(v.dtype), v)
    return o.reshape(B, S, ATTN_W)


def band_blocks(t, nb):
    B = t.shape[0]
    tp = jnp.pad(t, ((0, 0), (WINDOW, WINDOW), (0, 0), (0, 0))).reshape(B, nb + 2, WINDOW, N_KV_HEADS, HEAD_DIM)
    return jnp.concatenate([tp[:, :-2], tp[:, 1:-1], tp[:, 2:]], axis=2)


def latent_attention(q, k, v, sink, *, cos, sin, k_ctx, v_ctx):
    B, S = q.shape[:2]
    nb = S // WINDOW
    q = apply_rope(q, cos, sin)
    k = apply_rope(k, cos, sin)
    qb = q.reshape(B, nb, WINDOW, N_KV_HEADS, GROUP, HEAD_DIM)
    k_band = band_blocks(k, nb)
    v_band = band_blocks(v, nb)
    s_band = jnp.einsum('bnqkgd,bnckd->bkgnqc', qb, k_band).astype(jnp.float32) * ATTN_SCALE
    s_ctx = jnp.einsum('bnqkgd,bckd->bkgnqc', qb, k_ctx).astype(jnp.float32) * ATTN_SCALE
    blk = jnp.arange(nb)[:, None, None]
    qpos = blk * WINDOW + jnp.arange(WINDOW)[None, :, None]
    kpos = (blk - 1) * WINDOW + jnp.arange(3 * WINDOW)[None, None, :]
    mask = (jnp.abs(qpos - kpos) <= WINDOW) & (kpos >= 0) & (kpos < S)
    s_band = jnp.where(mask, s_band, -jnp.inf)
    sk = jnp.broadcast_to(sink.astype(jnp.float32).reshape(1, N_KV_HEADS, GROUP, 1, 1, 1), s_ctx.shape[:-1] + (1,))
    p = jax.nn.softmax(jnp.concatenate([s_band, s_ctx, sk], axis=-1), axis=-1)
    n_band = 3 * WINDOW
    n_ctx = k_ctx.shape[1]
    p_band = p[..., :n_band].astype(v.dtype)
    p_ctx = p[..., n_band:n_band + n_ctx].astype(v.dtype)
    o = (jnp.einsum('bkgnqc,bnckd->bnqkgd', p_band, v_band)
         + jnp.einsum('bkgnqc,bckd->bnqkgd', p_ctx, v_ctx))
    return o.reshape(B, S, ATTN_W)


def peer(h, w_query, sub_keys, u_tab, v_tab):
    B, S, D = h.shape
    xt = h.reshape((B * S) // PEER_BLOCK, PEER_BLOCK, D)

    def retrieve(xb):
        q = jnp.einsum('td,dhq->thq', xb, w_query)
        q1, q2 = jnp.split(q, 2, axis=-1)
        s1 = jnp.einsum('thq,hkq->thk', q1, sub_keys[:, 0]).astype(jnp.float32)
        s2 = jnp.einsum('thq,hkq->thk', q2, sub_keys[:, 1]).astype(jnp.float32)
        v1, i1 = lax.top_k(s1, PEER_TOPK)
        v2, i2 = lax.top_k(s2, PEER_TOPK)
        cand = (v1[..., :, None] + v2[..., None, :]).reshape(PEER_BLOCK, PEER_HEADS, PEER_TOPK * PEER_TOPK)
        cid = (i1[..., :, None] * N_KEYS + i2[..., None, :]).reshape(PEER_BLOCK, PEER_HEADS, PEER_TOPK * PEER_TOPK)
        best, pos = lax.top_k(cand, PEER_TOPK)
        eid = jnp.take_along_axis(cid, pos, axis=-1)
        g = jax.nn.softmax(best, axis=-1)
        u = jnp.take(u_tab, eid, axis=0)
        act = jax.nn.gelu(jnp.einsum('td,thkd->thk', xb, u).astype(jnp.float32))
        vv = jnp.take(v_tab, eid, axis=0)
        return jnp.einsum('thk,thkd->td', (g * act).astype(vv.dtype), vv)

    return lax.map(retrieve, xt).reshape(B, S, D)


def trunk_layer(x, mods, lw, attend, h0_f, h0_b):
    (g_mix, g_ffn, w_in, conv_w, conv_b, rg_w_a, rg_b_a, rg_w_i, rg_b_i, rg_lambda,
     attn_sink, w_out, pq, psk, pu, pv) = lw
    sh1, sc1, ga1, sh2, sc2, ga2 = mods
    h = modulated_norm(x, g_mix, sh1, sc1)
    q, k, v, xr, yg = mixer_inputs(h, w_in, conv_w, conv_b)
    o_att = attend(q, k, v, attn_sink)
    o_rnn, hf, hb = rglru_mixer(xr, yg, rg_w_a, rg_b_a, rg_w_i, rg_b_i, rg_lambda, h0_f, h0_b)
    mix = jnp.concatenate([o_att, o_rnn.astype(o_att.dtype)], axis=-1)
    x = x + ga1 * jnp.einsum('bsm,md->bsd', mix, w_out)
    h2 = modulated_norm(x, g_ffn, sh2, sc2)
    x = x + ga2 * peer(h2, pq, psk, pu, pv)
    return x, k, v, hf, hb


def setup_inputs(seed: int = 0) -> dict:
    key = jax.random.key(seed)
    ks = jax.random.split(key, 32)
    f32 = jnp.float32
    nrm = lambda k, shape, s: jax.random.normal(k, shape, f32) * s
    a0 = jax.random.uniform(ks[13], (DEPTH, 2, D_RNN), f32, 0.9, 0.999)
    sig = a0 ** (1.0 / RG_C)
    rg_lambda = jnp.log(sig) - jnp.log1p(-sig)
    return {
        "x_prompt": nrm(ks[0], (BATCH, SEQ, D_MODEL), 1.0),
        "x_sample": nrm(ks[1], (DEC_BATCH, DEC_SEQ, D_MODEL), 1.0),
        "cache_k": nrm(ks[2], (DEC_BATCH, DEPTH, PAST_LEN, N_KV_HEADS, HEAD_DIM), 1.0),
        "cache_v": nrm(ks[3], (DEC_BATCH, DEPTH, PAST_LEN, N_KV_HEADS, HEAD_DIM), 1.0),
        "state_rnn": nrm(ks[4], (DEC_BATCH, DEPTH, 2, D_RNN), 0.5),
        "c": nrm(ks[5], (DEC_BATCH, D_MODEL), 1.0),
        "c_ctx": nrm(ks[6], (D_MODEL,), 1.0),
        "w_mod": nrm(ks[7], (DEPTH, D_MODEL, N_MOD * D_MODEL), 0.5 * D_MODEL ** -0.5),
        "b_mod": nrm(ks[8], (DEPTH, N_MOD * D_MODEL), 0.02),
        "g_norm_mix": 1.0 + nrm(ks[9], (DEPTH, D_MODEL), 0.1),
        "g_norm_ffn": 1.0 + nrm(ks[10], (DEPTH, D_MODEL), 0.1),
        "w_in": nrm(ks[11], (DEPTH, D_MODEL, D_IN), D_MODEL ** -0.5),
        "conv_w": nrm(ks[12], (DEPTH, CONV_W, D_RNN), CONV_W ** -0.5),
        "conv_b": nrm(ks[14], (DEPTH, D_RNN), 0.02),
        "rg_w_a": nrm(ks[15], (DEPTH, 2, N_RNN_BLOCKS, RNN_BLOCK, RNN_BLOCK), RNN_BLOCK ** -0.5),
        "rg_b_a": nrm(ks[16], (DEPTH, 2, D_RNN), 0.02),
        "rg_w_i": nrm(ks[17], (DEPTH, 2, N_RNN_BLOCKS, RNN_BLOCK, RNN_BLOCK), RNN_BLOCK ** -0.5),
        "rg_b_i": nrm(ks[18], (DEPTH, 2, D_RNN), 0.02),
        "rg_lambda": rg_lambda,
        "attn_sink": nrm(ks[19], (DEPTH, N_HEADS), 0.5),
        "w_out": nrm(ks[20], (DEPTH, D_MIX, D_MODEL), D_MIX ** -0.5),
        "peer_w_query": nrm(ks[21], (DEPTH, D_MODEL, PEER_HEADS, PEER_DQ), D_MODEL ** -0.5),
        "peer_sub_keys": nrm(ks[22], (DEPTH, PEER_HEADS, 2, N_KEYS, PEER_DQ // 2), (PEER_DQ // 2) ** -0.5),
        "peer_u": nrm(ks[23], (DEPTH, N_EXPERTS, D_MODEL), D_MODEL ** -0.5),
        "peer_v": nrm(ks[24], (DEPTH, N_EXPERTS, D_MODEL), PEER_HEADS ** -0.5),
        "g_final": 1.0 + nrm(ks[25], (D_MODEL,), 0.1),
    }


def reference(x_prompt, x_sample, cache_k, cache_v, state_rnn, c, c_ctx, w_mod, b_mod,
              g_norm_mix, g_norm_ffn, w_in, conv_w, conv_b, rg_w_a, rg_b_a, rg_w_i, rg_b_i,
              rg_lambda, attn_sink, w_out, peer_w_query, peer_sub_keys, peer_u, peer_v, g_final):
    n_lat = x_sample.shape[1]
    rows = n_lat // GRID_W
    cos, sin = axial_rope(rows)
    xp, xs = x_prompt, x_sample
    h_zero = jnp.zeros((x_prompt.shape[0], D_RNN), jnp.float32)
    ks_out, vs_out, hs_out = [], [], []
    for l in range(DEPTH):
        lw = (g_norm_mix[l], g_norm_ffn[l], w_in[l], conv_w[l], conv_b[l], rg_w_a[l], rg_b_a[l],
              rg_w_i[l], rg_b_i[l], rg_lambda[l], attn_sink[l], w_out[l], peer_w_query[l],
              peer_sub_keys[l], peer_u[l], peer_v[l])
        mods_ctx = adaln_mods(c_ctx, w_mod[l], b_mod[l])
        mods_lat = [m[:, None, :] for m in adaln_mods(c, w_mod[l], b_mod[l])]
        xp, k_l, v_l, hf_l, hb_l = trunk_layer(xp, mods_ctx, lw, context_attention, h_zero, h_zero)
        ks_out.append(k_l)
        vs_out.append(v_l)
        hs_out.append(jnp.stack([hf_l, hb_l], axis=1))
        lat_attend = functools.partial(latent_attention, cos=cos, sin=sin,
                                       k_ctx=cache_k[:, l], v_ctx=cache_v[:, l])
        xs, _, _, _, _ = trunk_layer(xs, mods_lat, lw, lat_attend, state_rnn[:, l, 0], state_rnn[:, l, 1])
    y_prompt = rms_norm(xp, g_final)
    y_sample = rms_norm(xs, g_final)
    new_k = jnp.stack(ks_out, axis=1)
    new_v = jnp.stack(vs_out, axis=1)
    new_rnn = jnp.stack(hs_out, axis=1)
    return (y_prompt, y_sample, new_k, new_v, new_rnn)
```

```python
import functools
import math

import jax
import jax.numpy as jnp
from jax import lax
from jax.experimental import pallas as pl
from jax.experimental.pallas import tpu as pltpu

F32 = jnp.float32
BF16 = jnp.bfloat16

D_MODEL = 1024
GRID_W = 64
N_HEADS = 8
N_KV_HEADS = 2
HEAD_DIM = 64
ATTN_W = N_HEADS * HEAD_DIM
KV_W = N_KV_HEADS * HEAD_DIM
WINDOW = 128
ATTN_SCALE = HEAD_DIM ** -0.5
ROPE_BASE = 10000.0
ROPE_PAIRS_AXIS = HEAD_DIM // 4
D_RNN = 512
N_RNN_BLOCKS = 8
CONV_W = 4
CONV_LEFT = 2
RG_C = 8.0
D_IN = ATTN_W + 2 * KV_W + 2 * D_RNN
N_MOD = 6
PEER_HEADS = 8
N_KEYS = 128
PEER_DQ = 256
PEER_TOPK = 16
EPS = 1e-6

COL_K = ATTN_W
COL_V = ATTN_W + KV_W
COL_XR = ATTN_W + 2 * KV_W
COL_YG = COL_XR + D_RNN

LANES = 128
V7X_VMEM_LIMIT_BYTES = 56 * 1024 * 1024

MODS_COLS = 512
CTX_ROWS = 512
ROW_CHUNK = 256
RNN_HALF = D_RNN // 2
ROUTE_TOKENS = 256
EXPERT_TOKENS = 512
EXPERT_ROWS = 4
MODS_PAD_ROWS = 16


def _dot(a, b):
    return jnp.dot(a, b, preferred_element_type=F32)


def _dot_nt(a, b):
    return lax.dot_general(a, b, (((1,), (1,)), ((), ())), preferred_element_type=F32)


def _gelu(x):
    return 0.5 * x * (1.0 + jnp.tanh(math.sqrt(2.0 / math.pi) * (x + 0.044715 * (x * x * x))))


def _norm_mod(x, g, shift, scale):
    ms = jnp.mean(x * x, axis=-1, keepdims=True)
    return (x * lax.rsqrt(ms + EPS) * g) * (1.0 + scale) + shift


def _softplus(z):
    return jnp.maximum(z, 0.0) + jnp.log1p(jnp.exp(-jnp.abs(z)))


def _full(shape):
    return pl.BlockSpec(shape, lambda i: (0,) * len(shape), pipeline_mode=pl.Buffered(1))


def _mods_kernel(c_ref, w_ref, b_ref, o_ref):
    c = c_ref[...]
    s = (c * jax.nn.sigmoid(c)).astype(BF16)
    o_ref[...] = _dot(s, w_ref[...].astype(BF16)) + b_ref[...]


def _mods(cvec, w_mod, b_mod):
    n_out = w_mod.shape[1]
    return pl.pallas_call(
        _mods_kernel,
        grid=(n_out // MODS_COLS,),
        in_specs=[
            pl.BlockSpec((MODS_PAD_ROWS, D_MODEL), lambda j: (0, 0)),
            pl.BlockSpec((D_MODEL, MODS_COLS), lambda j: (0, j)),
            pl.BlockSpec((1, MODS_COLS), lambda j: (0, j)),
        ],
        out_specs=pl.BlockSpec((MODS_PAD_ROWS, MODS_COLS), lambda j: (0, j)),
        out_shape=jax.ShapeDtypeStruct((MODS_PAD_ROWS, n_out), F32),
        name="peer_trunk_mods",
    )(cvec, w_mod, b_mod.reshape(1, n_out))


def _split_mods(m):
    return [m[:, i * D_MODEL:(i + 1) * D_MODEL] for i in range(N_MOD)]


def _chunks(seqs, seq_len):
    return [(r, r // seq_len, r % seq_len) for r in range(0, seqs * seq_len, ROW_CHUNK)]


def _norm_to_scratch(x_ref, seqs, seq_len, mods, gmix_ref, hb_scr):
    for r, s, o in _chunks(seqs, seq_len):
        h = _norm_mod(x_ref[s, o:o + ROW_CHUNK, :], gmix_ref[...], mods[0], mods[1])
        hb_scr[r:r + ROW_CHUNK, :] = h.astype(BF16)


def _conv(xr, cw, cb):
    n = xr.shape[0]
    row = lax.broadcasted_iota(jnp.int32, xr.shape, 0)
    y = cb
    for j in range(CONV_W):
        off = j - CONV_LEFT
        if off == 0:
            xs = xr
        elif off < 0:
            xs = jnp.where(row >= -off, pltpu.roll(xr, -off, 0), 0.0)
        else:
            xs = jnp.where(row < n - off, pltpu.roll(xr, n - off, 0), 0.0)
        y = y + xs * cw[j:j + 1, :]
    return y


def _rnn_inputs_to_scratch(hb_scr, win_ref, cw_ref, cb_ref, xc_scr, gy_scr, seqs, seq_len):
    xr = _dot(hb_scr[...], win_ref[:, COL_XR:COL_YG])
    for s in range(seqs):
        r0 = s * seq_len
        xc_scr[r0:r0 + seq_len, :] = _conv(xr[r0:r0 + seq_len], cw_ref[...], cb_ref[...])
    gy_scr[...] = _gelu(_dot(hb_scr[...], win_ref[:, COL_YG:]))


def _scan(a, b, reverse):
    n = a.shape[0]
    row = lax.broadcasted_iota(jnp.int32, a.shape, 0)
    d = 1
    while d < n:
        if reverse:
            keep = row < n - d
            shift = n - d
        else:
            keep = row >= d
            shift = d
        a_s = jnp.where(keep, pltpu.roll(a, shift, 0), 1.0)
        b_s = jnp.where(keep, pltpu.roll(b, shift, 0), 0.0)
        b = a * b_s + b
        a = a * a_s
        d *= 2
    return a, b


def _rglru_to_mix(xc_scr, gy_scr, mix_scr, wg_ref, bg_ref, lam_ref, h0, r0, seq_len, state_out):
    sp = _softplus(-lam_ref[...])
    for hf in range(D_RNN // RNN_HALF):
        cols = slice(hf * RNN_HALF, (hf + 1) * RNN_HALF)
        xc = xc_scr[r0:r0 + seq_len, cols]
        xcb = xc.astype(BF16)
        total = None
        for d in range(2):
            r = jax.nn.sigmoid(_dot(xcb, wg_ref[2 * d, hf]) + bg_ref[2 * d:2 * d + 1, cols])
            gi = jax.nn.sigmoid(_dot(xcb, wg_ref[2 * d + 1, hf]) + bg_ref[2 * d + 1:2 * d + 2, cols])
            log_a = -RG_C * r * sp[d:d + 1, cols]
            a = jnp.exp(log_a)
            th = jnp.tanh(log_a)
            b = jnp.sqrt(-2.0 * th / (1.0 - th)) * gi * xc
            a_cum, b_cum = _scan(a, b, reverse=(d == 1))
            hd = a_cum * h0[d:d + 1, cols] + b_cum
            if state_out is not None:
                state_out(d, cols, hd)
            total = hd if total is None else total + hd
        mix_scr[r0:r0 + seq_len, ATTN_W + hf * RNN_HALF:ATTN_W + (hf + 1) * RNN_HALF] = (
            total * gy_scr[r0:r0 + seq_len, cols]).astype(BF16)


def _kv_variants(x):
    lane = lax.broadcasted_iota(jnp.int32, x.shape, 1)
    low = lane < HEAD_DIM
    xr = pltpu.roll(x, HEAD_DIM, 1)
    return [[jnp.where(low, x, 0.0), jnp.where(low, 0.0, xr)],
            [jnp.where(low, xr, 0.0), jnp.where(low, 0.0, x)]]


def _bf16_variants(x):
    return [[t.astype(BF16) for t in pair] for pair in _kv_variants(x)]


def _tail(x_ref, x1_ref, h2t_ref, mix_scr, wout_ref, mods, gffn_ref, seqs, seq_len):
    _, _, ga1, sh2, sc2, _ = mods
    for r, s, o in _chunks(seqs, seq_len):
        x1 = x_ref[s, o:o + ROW_CHUNK, :] + ga1 * _dot(mix_scr[r:r + ROW_CHUNK, :], wout_ref[...])
        x1_ref[s, o:o + ROW_CHUNK, :] = x1
        h2 = _norm_mod(x1, gffn_ref[...], sh2, sc2)
        h2t_ref[:, r:r + ROW_CHUNK] = h2.T.astype(BF16)


def _ctx_mixer_kernel(seqs, seq_len,
                      sink_ref, x_ref, mod_ref, gmix_ref, gffn_ref, win_ref, cw_ref, cb_ref,
                      wg_ref, bg_ref, lam_ref, wout_ref, h0_ref,
                      x1_ref, h2t_ref, k_ref, v_ref, rnn_ref,
                      hb_scr, xc_scr, gy_scr, mix_scr):
    mods = _split_mods(mod_ref[0])
    _norm_to_scratch(x_ref, seqs, seq_len, mods, gmix_ref, hb_scr)
    q_all = _dot(hb_scr[...], win_ref[:, :COL_K]).astype(BF16)
    kv_all = _dot(hb_scr[...], win_ref[:, COL_K:COL_XR])
    _rnn_inputs_to_scratch(hb_scr, win_ref, cw_ref, cb_ref, xc_scr, gy_scr, seqs, seq_len)

    for s in range(seqs):
        r0 = s * seq_len
        k = kv_all[r0:r0 + seq_len, :KV_W]
        v = kv_all[r0:r0 + seq_len, KV_W:]
        k_ref[s] = k
        v_ref[s] = v
        kvar = _bf16_variants(k)
        vvar = _bf16_variants(v)
        for j in range(N_HEADS // 2):
            kh = (2 * j) // (N_HEADS // N_KV_HEADS)
            qp = q_all[r0:r0 + seq_len, j * LANES:(j + 1) * LANES]
            o = None
            for var in range(2):
                sink = sink_ref[2 * j + var]
                sc = _dot_nt(qp, kvar[kh][var]) * ATTN_SCALE
                m = jnp.maximum(jnp.max(sc, axis=-1, keepdims=True), sink)
                e = jnp.exp(sc - m)
                den = jnp.sum(e, axis=-1, keepdims=True) + jnp.exp(sink - m)
                pv = _dot((e / den).astype(BF16), vvar[kh][var])
                o = pv if o is None else o + pv
            mix_scr[r0:r0 + seq_len, j * LANES:(j + 1) * LANES] = o.astype(BF16)

        def state_out(d, cols, hd, s=s):
            row = seq_len - 1 if d == 0 else 0
            rnn_ref[s, d:d + 1, cols] = hd[row:row + 1, :]

        _rglru_to_mix(xc_scr, gy_scr, mix_scr, wg_ref, bg_ref, lam_ref, h0_ref[s], r0, seq_len, state_out)

    _tail(x_ref, x1_ref, h2t_ref, mix_scr, wout_ref, mods, gffn_ref, seqs, seq_len)


def _mixer_scratch(rows):
    return [
        pltpu.VMEM((rows, D_MODEL), BF16),
        pltpu.VMEM((rows, D_RNN), F32),
        pltpu.VMEM((rows, D_RNN), F32),
        pltpu.VMEM((rows, ATTN_W + D_RNN), BF16),
    ]


def _ctx_mixer(x, mods_row, sink, g_mix, g_ffn, w_in, conv_w, conv_b, wg, bg, lam, w_out, h0):
    n_seq, seq_len, _ = x.shape
    seqs = CTX_ROWS // seq_len
    rows = seqs * seq_len
    kern = functools.partial(_ctx_mixer_kernel, seqs, seq_len)
    return pl.pallas_call(
        kern,
        grid=(n_seq // seqs,),
        in_specs=[
            pl.BlockSpec(memory_space=pltpu.SMEM),
            pl.BlockSpec((seqs, seq_len, D_MODEL), lambda i: (i, 0, 0)),
            _full((1, 1, N_MOD * D_MODEL)),
            _full((1, D_MODEL)),
            _full((1, D_MODEL)),
            _full((D_MODEL, D_IN)),
            _full((CONV_W, D_RNN)),
            _full((1, D_RNN)),
            _full((4, 2, RNN_HALF, RNN_HALF)),
            _full((4, D_RNN)),
            _full((2, D_RNN)),
            _full((ATTN_W + D_RNN, D_MODEL)),
            pl.BlockSpec((seqs, 2, D_RNN), lambda i: (i, 0, 0)),
        ],
        out_specs=[
            pl.BlockSpec((seqs, seq_len, D_MODEL), lambda i: (i, 0, 0)),
            pl.BlockSpec((D_MODEL, rows), lambda i: (0, i)),
            pl.BlockSpec((seqs, seq_len, KV_W), lambda i: (i, 0, 0)),
            pl.BlockSpec((seqs, seq_len, KV_W), lambda i: (i, 0, 0)),
            pl.BlockSpec((seqs, 2, D_RNN), lambda i: (i, 0, 0)),
        ],
        out_shape=[
            jax.ShapeDtypeStruct((n_seq, seq_len, D_MODEL), F32),
            jax.ShapeDtypeStruct((D_MODEL, n_seq * seq_len), BF16),
            jax.ShapeDtypeStruct((n_seq, seq_len, KV_W), F32),
            jax.ShapeDtypeStruct((n_seq, seq_len, KV_W), F32),
            jax.ShapeDtypeStruct((n_seq, 2, D_RNN), F32),
        ],
        scratch_shapes=_mixer_scratch(rows),
        compiler_params=pltpu.CompilerParams(
            dimension_semantics=("arbitrary",), vmem_limit_bytes=V7X_VMEM_LIMIT_BYTES),
        name="peer_trunk_ctx_mixer",
    )(sink, x, mods_row, g_mix, g_ffn, w_in, conv_w, conv_b, wg, bg, lam, w_out, h0)


def _lat_mixer_kernel(seq_len,
                      sink_ref, x_ref, mod_ref, gmix_ref, gffn_ref, win_ref, cw_ref, cb_ref,
                      wg_ref, bg_ref, lam_ref, wout_ref, h0_ref, ck_ref, cv_ref, cos_ref, sin_ref,
                      x1_ref, h2t_ref,
                      hb_scr, xc_scr, gy_scr, mix_scr, q_scr, k_scr, v_scr):
    n_blk = seq_len // WINDOW
    mods = _split_mods(mod_ref[0])
    _norm_to_scratch(x_ref, 1, seq_len, mods, gmix_ref, hb_scr)

    lane = lax.broadcasted_iota(jnp.int32, (seq_len, LANES), 1)
    first_half = jnp.bitwise_and(lane, HEAD_DIM - 1) < (HEAD_DIM // 2)

    def rope(t):
        partner = jnp.where(first_half, pltpu.roll(t, LANES - HEAD_DIM // 2, 1),
                            pltpu.roll(t, HEAD_DIM // 2, 1))
        return t * cos_ref[...] + partner * sin_ref[...]

    q = _dot(hb_scr[...], win_ref[:, :COL_K])
    for j in range(N_HEADS // 2):
        q_scr[:, j * LANES:(j + 1) * LANES] = rope(q[:, j * LANES:(j + 1) * LANES]).astype(BF16)
    kv = _dot(hb_scr[...], win_ref[:, COL_K:COL_XR])
    kvar = _bf16_variants(rope(kv[:, :KV_W]))
    vvar = _bf16_variants(kv[:, KV_W:])
    zeros = jnp.zeros((WINDOW, LANES), BF16)
    for kh in range(N_KV_HEADS):
        for var in range(2):
            for scr, val in ((k_scr, kvar), (v_scr, vvar)):
                scr[kh, var, 0:WINDOW, :] = zeros
                scr[kh, var, WINDOW:WINDOW + seq_len, :] = val[kh][var]
                scr[kh, var, WINDOW + seq_len:, :] = zeros
    ckv = _bf16_variants(ck_ref[0])
    cvv = _bf16_variants(cv_ref[0])
    _rnn_inputs_to_scratch(hb_scr, win_ref, cw_ref, cb_ref, xc_scr, gy_scr, 1, seq_len)

    band = 3 * WINDOW
    qi = lax.broadcasted_iota(jnp.int32, (WINDOW, band), 0)
    kj = lax.broadcasted_iota(jnp.int32, (WINDOW, band), 1)
    in_window = (kj - qi >= 0) & (kj - qi <= 2 * WINDOW)

    def block(n, carry):
        r0 = pl.multiple_of(n * WINDOW, WINDOW)
        kpos = kj + (n - 1) * WINDOW
        mask = in_window & (kpos >= 0) & (kpos < seq_len)
        for j in range(N_HEADS // 2):
            kh = (2 * j) // (N_HEADS // N_KV_HEADS)
            qp = q_scr[pl.ds(r0, WINDOW), j * LANES:(j + 1) * LANES]
            o = None
            for var in range(2):
                sink = sink_ref[2 * j + var]
                kw = k_scr[kh, var, pl.ds(r0, band), :]
                vw = v_scr[kh, var, pl.ds(r0, band), :]
                sb = jnp.where(mask, _dot_nt(qp, kw) * ATTN_SCALE, -jnp.inf)
                sc = _dot_nt(qp, ckv[kh][var]) * ATTN_SCALE
                m = jnp.maximum(jnp.maximum(jnp.max(sb, axis=-1, keepdims=True),
                                            jnp.max(sc, axis=-1, keepdims=True)), sink)
                eb = jnp.exp(sb - m)
                ec = jnp.exp(sc - m)
                den = (jnp.sum(eb, axis=-1, keepdims=True) + jnp.sum(ec, axis=-1, keepdims=True)
                       + jnp.exp(sink - m))
                inv = 1.0 / den
                pv = _dot((eb * inv).astype(BF16), vw) + _dot((ec * inv).astype(BF16), cvv[kh][var])
                o = pv if o is None else o + pv
            mix_scr[pl.ds(r0, WINDOW), j * LANES:(j + 1) * LANES] = o.astype(BF16)
        return carry

    lax.fori_loop(0, n_blk, block, 0)

    _rglru_to_mix(xc_scr, gy_scr, mix_scr, wg_ref, bg_ref, lam_ref, h0_ref[0], 0, seq_len, None)
    _tail(x_ref, x1_ref, h2t_ref, mix_scr, wout_ref, mods, gffn_ref, 1, seq_len)


def _lat_mixer(x, mods_rows, sink, g_mix, g_ffn, w_in, conv_w, conv_b, wg, bg, lam, w_out, h0,
               cache_k, cache_v, cos_t, sin_t):
    n_seq, seq_len, _ = x.shape
    n_ctx = cache_k.shape[1]
    kern = functools.partial(_lat_mixer_kernel, seq_len)
    return pl.pallas_call(
        kern,
        grid=(n_seq,),
        in_specs=[
            pl.BlockSpec(memory_space=pltpu.SMEM),
            pl.BlockSpec((1, seq_len, D_MODEL), lambda i: (i, 0, 0)),
            pl.BlockSpec((1, 1, N_MOD * D_MODEL), lambda i: (i, 0, 0)),
            _full((1, D_MODEL)),
            _full((1, D_MODEL)),
            _full((D_MODEL, D_IN)),
            _full((CONV_W, D_RNN)),
            _full((1, D_RNN)),
            _full((4, 2, RNN_HALF, RNN_HALF)),
            _full((4, D_RNN)),
            _full((2, D_RNN)),
            _full((ATTN_W + D_RNN, D_MODEL)),
            pl.BlockSpec((1, 2, D_RNN), lambda i: (i, 0, 0)),
            pl.BlockSpec((1, n_ctx, KV_W), lambda i: (i, 0, 0)),
            pl.BlockSpec((1, n_ctx, KV_W), lambda i: (i, 0, 0)),
            _full((seq_len, LANES)),
            _full((seq_len, LANES)),
        ],
        out_specs=[
            pl.BlockSpec((1, seq_len, D_MODEL), lambda i: (i, 0, 0)),
            pl.BlockSpec((D_MODEL, seq_len), lambda i: (0, i)),
        ],
        out_shape=[
            jax.ShapeDtypeStruct((n_seq, seq_len, D_MODEL), F32),
            jax.ShapeDtypeStruct((D_MODEL, n_seq * seq_len), BF16),
        ],
        scratch_shapes=_mixer_scratch(seq_len) + [
            pltpu.VMEM((seq_len, ATTN_W), BF16),
            pltpu.VMEM((N_KV_HEADS, 2, seq_len + 2 * WINDOW, LANES), BF16),
            pltpu.VMEM((N_KV_HEADS, 2, seq_len + 2 * WINDOW, LANES), BF16),
        ],
        compiler_params=pltpu.CompilerParams(
            dimension_semantics=("arbitrary",), vmem_limit_bytes=V7X_VMEM_LIMIT_BYTES),
        name="peer_trunk_lat_mixer",
    )(sink, x, mods_rows, g_mix, g_ffn, w_in, conv_w, conv_b, wg, bg, lam, w_out, h0,
      cache_k, cache_v, cos_t, sin_t)


def _staircase():
    return [(k1, k2) for k1 in range(PEER_TOPK) for k2 in range(PEER_TOPK)
            if (k1 + 1) * (k2 + 1) <= PEER_TOPK]


def _routing_kernel(tl,
                    h2t_ref, wq_ref, k1_ref, k2_ref,
                    a1_ref, n1_ref, b2_ref, r2_ref,
                    x1_scr, r1_scr, v1_scr, v2_scr):
    n_side = PEER_HEADS * (PEER_DQ // 2)
    qt = _dot(wq_ref[...], h2t_ref[...]).astype(BF16)
    x1_scr[...] = _dot(k1_ref[...], qt[:n_side]).reshape(N_KEYS, PEER_HEADS, tl)
    r1_scr[...] = jnp.full((N_KEYS, PEER_HEADS, tl), float(PEER_TOPK), F32)
    v2_scr[...] = jnp.zeros((PEER_TOPK, PEER_HEADS, tl), F32)

    iota1 = lax.broadcasted_iota(jnp.int32, (N_KEYS, PEER_HEADS, tl), 0)

    def pick1(k, carry):
        x = x1_scr[...]
        m = jnp.max(x, axis=0, keepdims=True)
        first = jnp.min(jnp.where(x == m, iota1, N_KEYS), axis=0, keepdims=True)
        sel = iota1 == first
        r1_scr[...] = jnp.where(sel, k.astype(F32), r1_scr[...])
        x1_scr[...] = jnp.where(sel, -jnp.inf, x)
        v1_scr[k] = m[0]
        return carry

    s1 = x1_scr[...]
    lax.fori_loop(0, PEER_TOPK, pick1, 0)

    iota2 = lax.broadcasted_iota(jnp.int32, (N_KEYS, tl), 0)
    sub = lax.broadcasted_iota(jnp.int32, (PEER_HEADS, tl), 0)
    s2 = []
    for h in range(PEER_HEADS):
        off = n_side + h * (PEER_DQ // 2)
        s2h = _dot(k2_ref[h], qt[off:off + PEER_DQ // 2])
        s2.append(s2h)

        def pick2(k, carry, h=h):
            x, r = carry
            m = jnp.max(x, axis=0, keepdims=True)
            first = jnp.min(jnp.where(x == m, iota2, N_KEYS), axis=0, keepdims=True)
            sel = iota2 == first
            v2_scr[k] = jnp.where(sub == h, jnp.broadcast_to(m, (PEER_HEADS, tl)), v2_scr[k])
            return jnp.where(sel, -jnp.inf, x), jnp.where(sel, k.astype(F32), r)

        _, r2h = lax.fori_loop(0, PEER_TOPK, pick2,
                               (s2h, jnp.full((N_KEYS, tl), float(PEER_TOPK), F32)))
        r2_ref[h] = r2h

    v1 = [v1_scr[k] for k in range(PEER_TOPK)]
    v2 = [v2_scr[k] for k in range(PEER_TOPK)]
    cand = _staircase()
    c = [v1[k1] + v2[k2] for (k1, k2) in cand]
    zero = jnp.zeros((PEER_HEADS, tl), F32)
    rank = [zero for _ in cand]
    for i, (a1, a2) in enumerate(cand):
        for j in range(i + 1, len(cand)):
            b1, b2 = cand[j]
            if a1 <= b1 and a2 <= b2:
                rank[j] = rank[j] + 1.0
            else:
                ge = jnp.where(c[i] >= c[j], 1.0, 0.0)
                rank[j] = rank[j] + ge
                rank[i] = rank[i] + (1.0 - ge)
    chosen = [jnp.where(r < float(PEER_TOPK), 1.0, 0.0) for r in rank]

    ea = [jnp.exp(v1[k] - v1[0]) for k in range(PEER_TOPK)]
    eb = [jnp.exp(v2[k] - v2[0]) for k in range(PEER_TOPK)]
    n_row = [zero for _ in range(PEER_TOPK)]
    z = zero
    for idx, (k1, k2) in enumerate(cand):
        n_row[k1] = n_row[k1] + chosen[idx]
        z = z + chosen[idx] * (ea[k1] * eb[k2])
    inv_z = 1.0 / z

    r1 = r1_scr[...]
    n1 = jnp.zeros((N_KEYS, PEER_HEADS, tl), F32)
    for k in range(PEER_TOPK):
        n1 = n1 + jnp.where(r1 == float(k), n_row[k][None], 0.0)
    n1_ref[...] = n1
    a1_ref[...] = jnp.exp(s1 - v1[0][None]) * inv_z[None]
    for h in range(PEER_HEADS):
        b2_ref[h] = jnp.exp(s2[h] - v2[0][h:h + 1, :])


def _routing(h2t, wq_t, k1_big, k2):
    n_tok = h2t.shape[1]
    tl = ROUTE_TOKENS
    kern = functools.partial(_routing_kernel, tl)
    out_shape = [
        jax.ShapeDtypeStruct((N_KEYS, PEER_HEADS, n_tok), F32),
        jax.ShapeDtypeStruct((N_KEYS, PEER_HEADS, n_tok), F32),
        jax.ShapeDtypeStruct((PEER_HEADS, N_KEYS, n_tok), F32),
        jax.ShapeDtypeStruct((PEER_HEADS, N_KEYS, n_tok), F32),
    ]
    return pl.pallas_call(
        kern,
        grid=(n_tok // tl,),
        in_specs=[
            pl.BlockSpec((D_MODEL, tl), lambda i: (0, i)),
            _full(wq_t.shape),
            _full(k1_big.shape),
            _full(k2.shape),
        ],
        out_specs=[
            pl.BlockSpec((N_KEYS, PEER_HEADS, tl), lambda i: (0, 0, i)),
            pl.BlockSpec((N_KEYS, PEER_HEADS, tl), lambda i: (0, 0, i)),
            pl.BlockSpec((PEER_HEADS, N_KEYS, tl), lambda i: (0, 0, i)),
            pl.BlockSpec((PEER_HEADS, N_KEYS, tl), lambda i: (0, 0, i)),
        ],
        out_shape=out_shape,
        scratch_shapes=[
            pltpu.VMEM((N_KEYS, PEER_HEADS, tl), F32),
            pltpu.VMEM((N_KEYS, PEER_HEADS, tl), F32),
            pltpu.VMEM((PEER_TOPK, PEER_HEADS, tl), F32),
            pltpu.VMEM((PEER_TOPK, PEER_HEADS, tl), F32),
        ],
        compiler_params=pltpu.CompilerParams(
            dimension_semantics=("arbitrary",), vmem_limit_bytes=V7X_VMEM_LIMIT_BYTES),
        name="peer_trunk_routing",
    )(h2t, wq_t, k1_big, k2)


def _experts_kernel(n_blocks,
                    h2t_ref, u_ref, vt_ref, a1_ref, n1_ref, b2_ref, r2_ref, x1_ref, ga2_ref, gfin_ref,
                    y_ref, acc_ref, w_scr):
    j = pl.program_id(1)

    @pl.when(j == 0)
    def _():
        acc_ref[...] = jnp.zeros_like(acc_ref)

    act = _gelu(_dot(u_ref[...], h2t_ref[...]))
    for r in range(EXPERT_ROWS):
        g = None
        for h in range(PEER_HEADS):
            n_b = n1_ref[r, h:h + 1, :]
            a_b = a1_ref[r, h:h + 1, :]
            t = jnp.where(r2_ref[h] < n_b, b2_ref[h], 0.0) * a_b
            g = t if g is None else g + t
        w_scr[r * N_KEYS:(r + 1) * N_KEYS, :] = (g * act[r * N_KEYS:(r + 1) * N_KEYS]).astype(BF16)
    acc_ref[...] += _dot(vt_ref[...], w_scr[...])

    @pl.when(j == n_blocks - 1)
    def _():
        x2 = x1_ref[...] + ga2_ref[0] * acc_ref[...].T
        ms = jnp.mean(x2 * x2, axis=-1, keepdims=True)
        y_ref[...] = x2 * lax.rsqrt(ms + EPS) * gfin_ref[...]


def _experts(h2t, u_bf, vt_bf, a1, n1, b2, r2, x1, ga2_tiles, g_final):
    n_tok = h2t.shape[1]
    tm = EXPERT_TOKENS
    eb = EXPERT_ROWS * N_KEYS
    n_blocks = N_KEYS // EXPERT_ROWS
    kern = functools.partial(_experts_kernel, n_blocks)
    return pl.pallas_call(
        kern,
        grid=(n_tok // tm, n_blocks),
        in_specs=[
            pl.BlockSpec((D_MODEL, tm), lambda i, j: (0, i)),
            pl.BlockSpec((eb, D_MODEL), lambda i, j: (j, 0)),
            pl.BlockSpec((D_MODEL, eb), lambda i, j: (0, j)),
            pl.BlockSpec((EXPERT_ROWS, PEER_HEADS, tm), lambda i, j: (j, 0, i)),
            pl.BlockSpec((EXPERT_ROWS, PEER_HEADS, tm), lambda i, j: (j, 0, i)),
            pl.BlockSpec((PEER_HEADS, N_KEYS, tm), lambda i, j: (0, 0, i)),
            pl.BlockSpec((PEER_HEADS, N_KEYS, tm), lambda i, j: (0, 0, i)),
            pl.BlockSpec((tm, D_MODEL), lambda i, j: (i, 0)),
            pl.BlockSpec((1, 1, D_MODEL), lambda i, j: (i, 0, 0)),
            pl.BlockSpec((1, D_MODEL), lambda i, j: (0, 0)),
        ],
        out_specs=pl.BlockSpec((tm, D_MODEL), lambda i, j: (i, 0)),
        out_shape=jax.ShapeDtypeStruct((n_tok, D_MODEL), F32),
        scratch_shapes=[
            pltpu.VMEM((D_MODEL, tm), F32),
            pltpu.VMEM((eb, tm), BF16),
        ],
        compiler_params=pltpu.CompilerParams(
            dimension_semantics=("arbitrary", "arbitrary"), vmem_limit_bytes=V7X_VMEM_LIMIT_BYTES),
        name="peer_trunk_experts",
    )(h2t, u_bf, vt_bf, a1, n1, b2, r2, x1, ga2_tiles, g_final)


def _rope_tables(seq_len):
    rows = seq_len // GRID_W
    row = jnp.repeat(jnp.arange(rows, dtype=F32), GRID_W)
    col = jnp.tile(jnp.arange(GRID_W, dtype=F32), rows)
    inv = ROPE_BASE ** (-jnp.arange(ROPE_PAIRS_AXIS, dtype=F32) / ROPE_PAIRS_AXIS)
    ang = jnp.concatenate([row[:, None] * inv, col[:, None] * inv], axis=-1)
    cos, sin = jnp.cos(ang), jnp.sin(ang)
    reps = LANES // HEAD_DIM
    cos_t = jnp.tile(jnp.concatenate([cos, cos], axis=-1), (1, reps))
    sin_t = jnp.tile(jnp.concatenate([-sin, sin], axis=-1), (1, reps))
    return cos_t, sin_t


def _gate_weights(w):
    eye = jnp.eye(N_RNN_BLOCKS, dtype=w.dtype)
    full = jnp.einsum("ncd,nm->ncmd", w, eye).reshape(D_RNN, D_RNN)
    return jnp.stack([full[:RNN_HALF, :RNN_HALF], full[RNN_HALF:, RNN_HALF:]])


def _ga2_tiles(ga2_rows, tokens_per_row, n_tok):
    n_tiles = n_tok // EXPERT_TOKENS
    tile_row = (jnp.arange(n_tiles) * EXPERT_TOKENS) // tokens_per_row
    return ga2_rows[tile_row][:, None, :]


def kernel(x_prompt, x_sample, cache_k, cache_v, state_rnn, c, c_ctx, w_mod, b_mod, g_norm_mix, g_norm_ffn, w_in, conv_w, conv_b, rg_w_a, rg_b_a, rg_w_i, rg_b_i, rg_lambda, attn_sink, w_out, peer_w_query, peer_sub_keys, peer_u, peer_v, g_final):
    assert w_mod.shape[0] == 1, "the expert stage fuses the final norm of a one-layer trunk"
    n_ctx_seq, ctx_len, _ = x_prompt.shape
    n_lat_seq, lat_len, _ = x_sample.shape
    n_past = cache_k.shape[2]
    n_ctx_tok = n_ctx_seq * ctx_len
    assert CTX_ROWS % ctx_len == 0 and n_ctx_seq % (CTX_ROWS // ctx_len) == 0
    assert ctx_len % ROW_CHUNK == 0 and lat_len % ROW_CHUNK == 0 and lat_len % WINDOW == 0
    assert lat_len % EXPERT_TOKENS == 0 and n_ctx_tok % EXPERT_TOKENS == 0
    assert n_lat_seq + 1 <= MODS_PAD_ROWS

    cos_t, sin_t = _rope_tables(lat_len)
    cvec = jnp.zeros((MODS_PAD_ROWS, D_MODEL), F32).at[:n_lat_seq].set(c).at[n_lat_seq].set(c_ctx)
    mods = _mods(cvec, w_mod[0], b_mod[0])
    mods_lat = mods[:n_lat_seq].reshape(n_lat_seq, 1, N_MOD * D_MODEL)
    mods_ctx = mods[n_lat_seq:n_lat_seq + 1].reshape(1, 1, N_MOD * D_MODEL)

    half_q = PEER_DQ // 2
    w_in_b = w_in[0].astype(BF16)
    w_out_b = w_out[0].astype(BF16)
    wg = jnp.stack([_gate_weights(rg_w_a[0, 0]), _gate_weights(rg_w_i[0, 0]),
                    _gate_weights(rg_w_a[0, 1]), _gate_weights(rg_w_i[0, 1])]).astype(BF16)
    bg = jnp.stack([rg_b_a[0, 0], rg_b_i[0, 0], rg_b_a[0, 1], rg_b_i[0, 1]])
    gmix = g_norm_mix[0].reshape(1, D_MODEL)
    gffn = g_norm_ffn[0].reshape(1, D_MODEL)
    gfin = g_final.reshape(1, D_MODEL)
    cb = conv_b[0].reshape(1, D_RNN)
    sink = attn_sink[0]
    wq = peer_w_query[0].reshape(D_MODEL, PEER_HEADS, 2, half_q)
    wq_t = jnp.transpose(wq, (2, 1, 3, 0)).reshape(2 * PEER_HEADS * half_q, D_MODEL).astype(BF16)
    eye_h = jnp.eye(PEER_HEADS, dtype=F32)
    k1_big = jnp.einsum("hkq,hg->khgq", peer_sub_keys[0, :, 0], eye_h).reshape(
        N_KEYS * PEER_HEADS, PEER_HEADS * half_q).astype(BF16)
    k2 = peer_sub_keys[0, :, 1].astype(BF16)
    u_bf = peer_u[0].astype(BF16)
    vt_bf = peer_v[0].T.astype(BF16)

    h0_ctx = jnp.zeros((n_ctx_seq, 2, D_RNN), F32)
    x1_c, h2t_c, k_l, v_l, rnn_l = _ctx_mixer(
        x_prompt, mods_ctx, sink, gmix, gffn, w_in_b, conv_w[0], cb, wg, bg, rg_lambda[0], w_out_b, h0_ctx)

    ck = cache_k[:, 0].reshape(n_lat_seq, n_past, KV_W)
    cv = cache_v[:, 0].reshape(n_lat_seq, n_past, KV_W)
    x1_s, h2t_s = _lat_mixer(
        x_sample, mods_lat, sink, gmix, gffn, w_in_b, conv_w[0], cb, wg, bg, rg_lambda[0], w_out_b,
        state_rnn[:, 0], ck, cv, cos_t, sin_t)

    ga2_off = (N_MOD - 1) * D_MODEL
    outs = []
    for x1, h2t, ga2_rows, per_row in (
            (x1_c, h2t_c, mods[n_lat_seq:n_lat_seq + 1, ga2_off:], n_ctx_tok),
            (x1_s, h2t_s, mods[:n_lat_seq, ga2_off:], lat_len)):
        n_tok = h2t.shape[1]
        a1, n1, b2, r2 = _routing(h2t, wq_t, k1_big, k2)
        ga2 = _ga2_tiles(ga2_rows, per_row, n_tok)
        outs.append(_experts(h2t, u_bf, vt_bf, a1, n1, b2, r2, x1.reshape(n_tok, D_MODEL), ga2, gfin))
    y_prompt = outs[0].reshape(n_ctx_seq, ctx_len, D_MODEL)
    y_sample = outs[1].reshape(n_lat_seq, lat_len, D_MODEL)

    new_k = k_l.reshape(n_ctx_seq, 1, ctx_len, N_KV_HEADS, HEAD_DIM)
    new_v = v_l.reshape(n_ctx_seq, 1, ctx_len, N_KV_HEADS, HEAD_DIM)
    new_rnn = rnn_l[:, None]
    return (y_prompt, y_sample, new_k, new_v, new_rnn)
```

```python
import functools
import math

import jax
import jax.numpy as jnp
from jax import lax
from jax.experimental import pallas as pl
from jax.experimental.pallas import tpu as pltpu

F32 = jnp.float32
BF16 = jnp.bfloat16

D_MODEL = 1024
GRID_W = 64
N_HEADS = 8
N_KV_HEADS = 2
HEAD_DIM = 64
ATTN_W = N_HEADS * HEAD_DIM
KV_W = N_KV_HEADS * HEAD_DIM
WINDOW = 128
ATTN_SCALE = HEAD_DIM ** -0.5
ROPE_BASE = 10000.0
ROPE_PAIRS_AXIS = HEAD_DIM // 4
D_RNN = 512
N_RNN_BLOCKS = 8
CONV_W = 4
CONV_LEFT = 2
RG_C = 8.0
D_IN = ATTN_W + 2 * KV_W + 2 * D_RNN
N_MOD = 6
PEER_HEADS = 8
N_KEYS = 128
PEER_DQ = 256
PEER_TOPK = 16
EPS = 1e-6

COL_K = ATTN_W
COL_V = ATTN_W + KV_W
COL_XR = ATTN_W + 2 * KV_W
COL_YG = COL_XR + D_RNN

LANES = 128
V7X_VMEM_LIMIT_BYTES = 56 * 1024 * 1024

MODS_COLS = 512
CTX_ROWS = 512
ROW_CHUNK = 256
RNN_HALF = D_RNN // 2
ROUTE_TOKENS = 256
EXPERT_TOKENS = 512
EXPERT_ROWS = 16
EXPERT_SUB_ROWS = 4
BF16_SUBLANES = 16
MODS_PAD_ROWS = 16


def _dot(a, b):
    return jnp.dot(a, b, preferred_element_type=F32)


def _dot_nt(a, b):
    return lax.dot_general(a, b, (((1,), (1,)), ((), ())), preferred_element_type=F32)


def _gelu(x):
    return 0.5 * x * (1.0 + jnp.tanh(math.sqrt(2.0 / math.pi) * (x + 0.044715 * (x * x * x))))


def _norm_mod(x, g, shift, scale):
    ms = jnp.mean(x * x, axis=-1, keepdims=True)
    return (x * lax.rsqrt(ms + EPS) * g) * (1.0 + scale) + shift


def _softplus(z):
    return jnp.maximum(z, 0.0) + jnp.log1p(jnp.exp(-jnp.abs(z)))


def _full(shape):
    return pl.BlockSpec(shape, lambda i: (0,) * len(shape), pipeline_mode=pl.Buffered(1))


def _mods_kernel(c_ref, w_ref, b_ref, o_ref):
    c = c_ref[...]
    s = (c * jax.nn.sigmoid(c)).astype(BF16)
    o_ref[...] = _dot(s, w_ref[...].astype(BF16)) + b_ref[...]


def _mods(cvec, w_mod, b_mod):
    n_out = w_mod.shape[1]
    return pl.pallas_call(
        _mods_kernel,
        grid=(n_out // MODS_COLS,),
        in_specs=[
            pl.BlockSpec((MODS_PAD_ROWS, D_MODEL), lambda j: (0, 0)),
            pl.BlockSpec((D_MODEL, MODS_COLS), lambda j: (0, j)),
            pl.BlockSpec((1, MODS_COLS), lambda j: (0, j)),
        ],
        out_specs=pl.BlockSpec((MODS_PAD_ROWS, MODS_COLS), lambda j: (0, j)),
        out_shape=jax.ShapeDtypeStruct((MODS_PAD_ROWS, n_out), F32),
        name="peer_trunk_mods",
    )(cvec, w_mod, b_mod.reshape(1, n_out))


def _split_mods(m):
    return [m[:, i * D_MODEL:(i + 1) * D_MODEL] for i in range(N_MOD)]


def _chunks(seqs, seq_len):
    return [(r, r // seq_len, r % seq_len) for r in range(0, seqs * seq_len, ROW_CHUNK)]


def _norm_to_scratch(x_ref, seqs, seq_len, mods, gmix_ref, hb_scr):
    for r, s, o in _chunks(seqs, seq_len):
        h = _norm_mod(x_ref[s, o:o + ROW_CHUNK, :], gmix_ref[...], mods[0], mods[1])
        hb_scr[r:r + ROW_CHUNK, :] = h.astype(BF16)


def _conv(xr, cw, cb):
    n = xr.shape[0]
    row = lax.broadcasted_iota(jnp.int32, xr.shape, 0)
    y = cb
    for j in range(CONV_W):
        off = j - CONV_LEFT
        if off == 0:
            xs = xr
        elif off < 0:
            xs = jnp.where(row >= -off, pltpu.roll(xr, -off, 0), 0.0)
        else:
            xs = jnp.where(row < n - off, pltpu.roll(xr, n - off, 0), 0.0)
        y = y + xs * cw[j:j + 1, :]
    return y


def _rnn_inputs_to_scratch(hb_scr, win_ref, cw_ref, cb_ref, xc_scr, gy_scr, seqs, seq_len):
    xr = _dot(hb_scr[...], win_ref[:, COL_XR:COL_YG])
    for s in range(seqs):
        r0 = s * seq_len
        xc_scr[r0:r0 + seq_len, :] = _conv(xr[r0:r0 + seq_len], cw_ref[...], cb_ref[...])
    gy_scr[...] = _gelu(_dot(hb_scr[...], win_ref[:, COL_YG:]))


def _scan(a, b, reverse):
    n = a.shape[0]
    row = lax.broadcasted_iota(jnp.int32, a.shape, 0)
    d = 1
    while d < n:
        if reverse:
            keep = row < n - d
            shift = n - d
        else:
            keep = row >= d
            shift = d
        a_s = jnp.where(keep, pltpu.roll(a, shift, 0), 1.0)
        b_s = jnp.where(keep, pltpu.roll(b, shift, 0), 0.0)
        b = a * b_s + b
        a = a * a_s
        d *= 2
    return a, b


def _rglru_to_mix(xc_scr, gy_scr, mix_scr, wg_ref, bg_ref, lam_ref, h0, r0, seq_len, state_out):
    sp = _softplus(-lam_ref[...])
    for hf in range(D_RNN // RNN_HALF):
        cols = slice(hf * RNN_HALF, (hf + 1) * RNN_HALF)
        xc = xc_scr[r0:r0 + seq_len, cols]
        xcb = xc.astype(BF16)
        total = None
        for d in range(2):
            r = jax.nn.sigmoid(_dot(xcb, wg_ref[2 * d, hf]) + bg_ref[2 * d:2 * d + 1, cols])
            gi = jax.nn.sigmoid(_dot(xcb, wg_ref[2 * d + 1, hf]) + bg_ref[2 * d + 1:2 * d + 2, cols])
            log_a = -RG_C * r * sp[d:d + 1, cols]
            a = jnp.exp(log_a)
            th = jnp.tanh(log_a)
            b = jnp.sqrt(-2.0 * th / (1.0 - th)) * gi * xc
            a_cum, b_cum = _scan(a, b, reverse=(d == 1))
            hd = a_cum * h0[d:d + 1, cols] + b_cum
            if state_out is not None:
                state_out(d, cols, hd)
            total = hd if total is None else total + hd
        mix_scr[r0:r0 + seq_len, ATTN_W + hf * RNN_HALF:ATTN_W + (hf + 1) * RNN_HALF] = (
            total * gy_scr[r0:r0 + seq_len, cols]).astype(BF16)


def _kv_variants(x):
    lane = lax.broadcasted_iota(jnp.int32, x.shape, 1)
    low = lane < HEAD_DIM
    xr = pltpu.roll(x, HEAD_DIM, 1)
    return [[jnp.where(low, x, 0.0), jnp.where(low, 0.0, xr)],
            [jnp.where(low, xr, 0.0), jnp.where(low, 0.0, x)]]


def _bf16_variants(x):
    return [[t.astype(BF16) for t in pair] for pair in _kv_variants(x)]


def _tail(x_ref, x1_ref, h2t_ref, mix_scr, wout_ref, mods, gffn_ref, seqs, seq_len):
    _, _, ga1, sh2, sc2, _ = mods
    for r, s, o in _chunks(seqs, seq_len):
        x1 = x_ref[s, o:o + ROW_CHUNK, :] + ga1 * _dot(mix_scr[r:r + ROW_CHUNK, :], wout_ref[...])
        x1_ref[s, o:o + ROW_CHUNK, :] = x1
        h2 = _norm_mod(x1, gffn_ref[...], sh2, sc2)
        h2t_ref[:, r:r + ROW_CHUNK] = h2.T.astype(BF16)


def _ctx_mixer_kernel(seqs, seq_len,
                      sink_ref, x_ref, mod_ref, gmix_ref, gffn_ref, win_ref, cw_ref, cb_ref,
                      wg_ref, bg_ref, lam_ref, wout_ref, h0_ref,
                      x1_ref, h2t_ref, k_ref, v_ref, rnn_ref,
                      hb_scr, xc_scr, gy_scr, mix_scr):
    mods = _split_mods(mod_ref[0])
    _norm_to_scratch(x_ref, seqs, seq_len, mods, gmix_ref, hb_scr)
    q_all = _dot(hb_scr[...], win_ref[:, :COL_K]).astype(BF16)
    kv_all = _dot(hb_scr[...], win_ref[:, COL_K:COL_XR])
    _rnn_inputs_to_scratch(hb_scr, win_ref, cw_ref, cb_ref, xc_scr, gy_scr, seqs, seq_len)

    for s in range(seqs):
        r0 = s * seq_len
        k = kv_all[r0:r0 + seq_len, :KV_W]
        v = kv_all[r0:r0 + seq_len, KV_W:]
        k_ref[s] = k
        v_ref[s] = v
        kvar = _bf16_variants(k)
        vvar = _bf16_variants(v)
        for j in range(N_HEADS // 2):
            kh = (2 * j) // (N_HEADS // N_KV_HEADS)
            qp = q_all[r0:r0 + seq_len, j * LANES:(j + 1) * LANES]
            o = None
            for var in range(2):
                sink = sink_ref[2 * j + var]
                sc = _dot_nt(qp, kvar[kh][var]) * ATTN_SCALE
                m = jnp.maximum(jnp.max(sc, axis=-1, keepdims=True), sink)
                e = jnp.exp(sc - m)
                den = jnp.sum(e, axis=-1, keepdims=True) + jnp.exp(sink - m)
                pv = _dot((e / den).astype(BF16), vvar[kh][var])
                o = pv if o is None else o + pv
            mix_scr[r0:r0 + seq_len, j * LANES:(j + 1) * LANES] = o.astype(BF16)

        def state_out(d, cols, hd, s=s):
            row = seq_len - 1 if d == 0 else 0
            rnn_ref[s, d:d + 1, cols] = hd[row:row + 1, :]

        _rglru_to_mix(xc_scr, gy_scr, mix_scr, wg_ref, bg_ref, lam_ref, h0_ref[s], r0, seq_len, state_out)

    _tail(x_ref, x1_ref, h2t_ref, mix_scr, wout_ref, mods, gffn_ref, seqs, seq_len)


def _mixer_scratch(rows):
    return [
        pltpu.VMEM((rows, D_MODEL), BF16),
        pltpu.VMEM((rows, D_RNN), F32),
        pltpu.VMEM((rows, D_RNN), F32),
        pltpu.VMEM((rows, ATTN_W + D_RNN), BF16),
    ]


def _ctx_mixer(x, mods_row, sink, g_mix, g_ffn, w_in, conv_w, conv_b, wg, bg, lam, w_out, h0):
    n_seq, seq_len, _ = x.shape
    seqs = CTX_ROWS // seq_len
    rows = seqs * seq_len
    kern = functools.partial(_ctx_mixer_kernel, seqs, seq_len)
    return pl.pallas_call(
        kern,
        grid=(n_seq // seqs,),
        in_specs=[
            pl.BlockSpec(memory_space=pltpu.SMEM),
            pl.BlockSpec((seqs, seq_len, D_MODEL), lambda i: (i, 0, 0)),
            _full((1, 1, N_MOD * D_MODEL)),
            _full((1, D_MODEL)),
            _full((1, D_MODEL)),
            _full((D_MODEL, D_IN)),
            _full((CONV_W, D_RNN)),
            _full((1, D_RNN)),
            _full((4, 2, RNN_HALF, RNN_HALF)),
            _full((4, D_RNN)),
            _full((2, D_RNN)),
            _full((ATTN_W + D_RNN, D_MODEL)),
            pl.BlockSpec((seqs, 2, D_RNN), lambda i: (i, 0, 0)),
        ],
        out_specs=[
            pl.BlockSpec((seqs, seq_len, D_MODEL), lambda i: (i, 0, 0)),
            pl.BlockSpec((D_MODEL, rows), lambda i: (0, i)),
            pl.BlockSpec((seqs, seq_len, KV_W), lambda i: (i, 0, 0)),
            pl.BlockSpec((seqs, seq_len, KV_W), lambda i: (i, 0, 0)),
            pl.BlockSpec((seqs, 2, D_RNN), lambda i: (i, 0, 0)),
        ],
        out_shape=[
            jax.ShapeDtypeStruct((n_seq, seq_len, D_MODEL), F32),
            jax.ShapeDtypeStruct((D_MODEL, n_seq * seq_len), BF16),
            jax.ShapeDtypeStruct((n_seq, seq_len, KV_W), F32),
            jax.ShapeDtypeStruct((n_seq, seq_len, KV_W), F32),
            jax.ShapeDtypeStruct((n_seq, 2, D_RNN), F32),
        ],
        scratch_shapes=_mixer_scratch(rows),
        compiler_params=pltpu.CompilerParams(
            dimension_semantics=("arbitrary",), vmem_limit_bytes=V7X_VMEM_LIMIT_BYTES),
        name="peer_trunk_ctx_mixer",
    )(sink, x, mods_row, g_mix, g_ffn, w_in, conv_w, conv_b, wg, bg, lam, w_out, h0)


def _lat_mixer_kernel(seq_len,
                      sink_ref, x_ref, mod_ref, gmix_ref, gffn_ref, win_ref, cw_ref, cb_ref,
                      wg_ref, bg_ref, lam_ref, wout_ref, h0_ref, ck_ref, cv_ref, cos_ref, sin_ref,
                      x1_ref, h2t_ref,
                      hb_scr, xc_scr, gy_scr, mix_scr, q_scr, k_scr, v_scr):
    n_blk = seq_len // WINDOW
    mods = _split_mods(mod_ref[0])
    _norm_to_scratch(x_ref, 1, seq_len, mods, gmix_ref, hb_scr)

    lane = lax.broadcasted_iota(jnp.int32, (seq_len, LANES), 1)
    first_half = jnp.bitwise_and(lane, HEAD_DIM - 1) < (HEAD_DIM // 2)

    def rope(t):
        partner = jnp.where(first_half, pltpu.roll(t, LANES - HEAD_DIM // 2, 1),
                            pltpu.roll(t, HEAD_DIM // 2, 1))
        return t * cos_ref[...] + partner * sin_ref[...]

    q = _dot(hb_scr[...], win_ref[:, :COL_K])
    for j in range(N_HEADS // 2):
        q_scr[:, j * LANES:(j + 1) * LANES] = rope(q[:, j * LANES:(j + 1) * LANES]).astype(BF16)
    kv = _dot(hb_scr[...], win_ref[:, COL_K:COL_XR])
    kvar = _bf16_variants(rope(kv[:, :KV_W]))
    vvar = _bf16_variants(kv[:, KV_W:])
    zeros = jnp.zeros((WINDOW, LANES), BF16)
    for kh in range(N_KV_HEADS):
        for var in range(2):
            for scr, val in ((k_scr, kvar), (v_scr, vvar)):
                scr[kh, var, 0:WINDOW, :] = zeros
                scr[kh, var, WINDOW:WINDOW + seq_len, :] = val[kh][var]
                scr[kh, var, WINDOW + seq_len:, :] = zeros
    ckv = _bf16_variants(ck_ref[0])
    cvv = _bf16_variants(cv_ref[0])
    _rnn_inputs_to_scratch(hb_scr, win_ref, cw_ref, cb_ref, xc_scr, gy_scr, 1, seq_len)

    band = 3 * WINDOW
    qi = lax.broadcasted_iota(jnp.int32, (WINDOW, band), 0)
    kj = lax.broadcasted_iota(jnp.int32, (WINDOW, band), 1)
    in_window = (kj - qi >= 0) & (kj - qi <= 2 * WINDOW)

    def block(n, carry):
        r0 = pl.multiple_of(n * WINDOW, WINDOW)
        kpos = kj + (n - 1) * WINDOW
        mask = in_window & (kpos >= 0) & (kpos < seq_len)
        for j in range(N_HEADS // 2):
            kh = (2 * j) // (N_HEADS // N_KV_HEADS)
            qp = q_scr[pl.ds(r0, WINDOW), j * LANES:(j + 1) * LANES]
            o = None
            for var in range(2):
                sink = sink_ref[2 * j + var]
                kw = k_scr[kh, var, pl.ds(r0, band), :]
                vw = v_scr[kh, var, pl.ds(r0, band), :]
                sb = jnp.where(mask, _dot_nt(qp, kw) * ATTN_SCALE, -jnp.inf)
                sc = _dot_nt(qp, ckv[kh][var]) * ATTN_SCALE
                m = jnp.maximum(jnp.maximum(jnp.max(sb, axis=-1, keepdims=True),
                                            jnp.max(sc, axis=-1, keepdims=True)), sink)
                eb = jnp.exp(sb - m)
                ec = jnp.exp(sc - m)
                den = (jnp.sum(eb, axis=-1, keepdims=True) + jnp.sum(ec, axis=-1, keepdims=True)
                       + jnp.exp(sink - m))
                inv = 1.0 / den
                pv = _dot((eb * inv).astype(BF16), vw) + _dot((ec * inv).astype(BF16), cvv[kh][var])
                o = pv if o is None else o + pv
            mix_scr[pl.ds(r0, WINDOW), j * LANES:(j + 1) * LANES] = o.astype(BF16)
        return carry

    lax.fori_loop(0, n_blk, block, 0)

    _rglru_to_mix(xc_scr, gy_scr, mix_scr, wg_ref, bg_ref, lam_ref, h0_ref[0], 0, seq_len, None)
    _tail(x_ref, x1_ref, h2t_ref, mix_scr, wout_ref, mods, gffn_ref, 1, seq_len)


def _lat_mixer(x, mods_rows, sink, g_mix, g_ffn, w_in, conv_w, conv_b, wg, bg, lam, w_out, h0,
               cache_k, cache_v, cos_t, sin_t):
    n_seq, seq_len, _ = x.shape
    n_ctx = cache_k.shape[1]
    kern = functools.partial(_lat_mixer_kernel, seq_len)
    return pl.pallas_call(
        kern,
        grid=(n_seq,),
        in_specs=[
            pl.BlockSpec(memory_space=pltpu.SMEM),
            pl.BlockSpec((1, seq_len, D_MODEL), lambda i: (i, 0, 0)),
            pl.BlockSpec((1, 1, N_MOD * D_MODEL), lambda i: (i, 0, 0)),
            _full((1, D_MODEL)),
            _full((1, D_MODEL)),
            _full((D_MODEL, D_IN)),
            _full((CONV_W, D_RNN)),
            _full((1, D_RNN)),
            _full((4, 2, RNN_HALF, RNN_HALF)),
            _full((4, D_RNN)),
            _full((2, D_RNN)),
            _full((ATTN_W + D_RNN, D_MODEL)),
            pl.BlockSpec((1, 2, D_RNN), lambda i: (i, 0, 0)),
            pl.BlockSpec((1, n_ctx, KV_W), lambda i: (i, 0, 0)),
            pl.BlockSpec((1, n_ctx, KV_W), lambda i: (i, 0, 0)),
            _full((seq_len, LANES)),
            _full((seq_len, LANES)),
        ],
        out_specs=[
            pl.BlockSpec((1, seq_len, D_MODEL), lambda i: (i, 0, 0)),
            pl.BlockSpec((D_MODEL, seq_len), lambda i: (0, i)),
        ],
        out_shape=[
            jax.ShapeDtypeStruct((n_seq, seq_len, D_MODEL), F32),
            jax.ShapeDtypeStruct((D_MODEL, n_seq * seq_len), BF16),
        ],
        scratch_shapes=_mixer_scratch(seq_len) + [
            pltpu.VMEM((seq_len, ATTN_W), BF16),
            pltpu.VMEM((N_KV_HEADS, 2, seq_len + 2 * WINDOW, LANES), BF16),
            pltpu.VMEM((N_KV_HEADS, 2, seq_len + 2 * WINDOW, LANES), BF16),
        ],
        compiler_params=pltpu.CompilerParams(
            dimension_semantics=("arbitrary",), vmem_limit_bytes=V7X_VMEM_LIMIT_BYTES),
        name="peer_trunk_lat_mixer",
    )(sink, x, mods_rows, g_mix, g_ffn, w_in, conv_w, conv_b, wg, bg, lam, w_out, h0,
      cache_k, cache_v, cos_t, sin_t)


def _staircase():
    return [(k1, k2) for k1 in range(PEER_TOPK) for k2 in range(PEER_TOPK)
            if (k1 + 1) * (k2 + 1) <= PEER_TOPK]


def _routing_kernel(tl,
                    h2t_ref, wq_ref, k1_ref, k2_ref,
                    a1_ref, n1_ref, b2_ref, r2_ref,
                    x1_scr, i1_scr, v1_scr, x2_scr, r2_scr, v2_scr):
    n_side = PEER_HEADS * (PEER_DQ // 2)
    half_q = PEER_DQ // 2
    qt = _dot(wq_ref[...], h2t_ref[...]).astype(BF16)

    def scores1():
        return _dot(k1_ref[...], qt[:n_side]).reshape(N_KEYS, PEER_HEADS, tl)

    def scores2(h):
        return _dot(k2_ref[h], qt[n_side + h * half_q:n_side + (h + 1) * half_q])

    x1_scr[...] = scores1()
    for h in range(PEER_HEADS):
        x2_scr[h] = scores2(h)
    r2_scr[...] = jnp.full((PEER_HEADS, N_KEYS, tl), float(PEER_TOPK), F32)

    iota1 = lax.broadcasted_iota(jnp.int32, (N_KEYS, PEER_HEADS, tl), 0)
    iota2 = lax.broadcasted_iota(jnp.int32, (N_KEYS, tl), 0)
    sub = lax.broadcasted_iota(jnp.int32, (PEER_HEADS, tl), 0)

    def pick(k, carry):
        x = x1_scr[...]
        m = jnp.max(x, axis=0, keepdims=True)
        first = jnp.min(jnp.where(x == m, iota1, N_KEYS), axis=0, keepdims=True)
        x1_scr[...] = jnp.where(iota1 == first, -jnp.inf, x)
        v1_scr[k] = m[0]
        i1_scr[k] = first[0]
        kf = k.astype(F32)
        v2k = jnp.zeros((PEER_HEADS, tl), F32)
        for h in range(PEER_HEADS):
            xh = x2_scr[h]
            mh = jnp.max(xh, axis=0, keepdims=True)
            fh = jnp.min(jnp.where(xh == mh, iota2, N_KEYS), axis=0, keepdims=True)
            sel = iota2 == fh
            x2_scr[h] = jnp.where(sel, -jnp.inf, xh)
            r2_scr[h] = jnp.where(sel, kf, r2_scr[h])
            v2k = jnp.where(sub == h, jnp.broadcast_to(mh, (PEER_HEADS, tl)), v2k)
        v2_scr[k] = v2k
        return carry

    lax.fori_loop(0, PEER_TOPK, pick, 0)

    v1 = [v1_scr[k] for k in range(PEER_TOPK)]
    v2 = [v2_scr[k] for k in range(PEER_TOPK)]
    cand = _staircase()
    c = [v1[k1] + v2[k2] for (k1, k2) in cand]
    zero = jnp.zeros((PEER_HEADS, tl), F32)
    rank = [zero for _ in cand]
    for i, (a1, a2) in enumerate(cand):
        for j in range(i + 1, len(cand)):
            b1, b2 = cand[j]
            if a1 <= b1 and a2 <= b2:
                rank[j] = rank[j] + 1.0
            else:
                ge = jnp.where(c[i] >= c[j], 1.0, 0.0)
                rank[j] = rank[j] + ge
                rank[i] = rank[i] + (1.0 - ge)
    chosen = [jnp.where(r < float(PEER_TOPK), 1.0, 0.0) for r in rank]

    ea = [jnp.exp(v1[k] - v1[0]) for k in range(PEER_TOPK)]
    eb = [jnp.exp(v2[k] - v2[0]) for k in range(PEER_TOPK)]
    n_row = [zero for _ in range(PEER_TOPK)]
    z = zero
    for idx, (k1, k2) in enumerate(cand):
        n_row[k1] = n_row[k1] + chosen[idx]
        z = z + chosen[idx] * (ea[k1] * eb[k2])
    inv_z = 1.0 / z

    n1 = jnp.zeros((N_KEYS, PEER_HEADS, tl), F32)
    for k in range(PEER_TOPK):
        n1 = n1 + jnp.where(iota1 == i1_scr[k][None], n_row[k][None], 0.0)
    n1_ref[...] = n1
    a1_ref[...] = jnp.exp(scores1() - v1[0][None]) * inv_z[None]
    for h in range(PEER_HEADS):
        b2_ref[h] = jnp.exp(scores2(h) - v2[0][h:h + 1, :]).astype(BF16)
    r2_ref[...] = r2_scr[...].astype(BF16)


def _routing(h2t, wq_t, k1_big, k2):
    n_tok = h2t.shape[1]
    tl = ROUTE_TOKENS
    kern = functools.partial(_routing_kernel, tl)
    out_shape = [
        jax.ShapeDtypeStruct((N_KEYS, PEER_HEADS, n_tok), F32),
        jax.ShapeDtypeStruct((N_KEYS, PEER_HEADS, n_tok), F32),
        jax.ShapeDtypeStruct((PEER_HEADS, N_KEYS, n_tok), BF16),
        jax.ShapeDtypeStruct((PEER_HEADS, N_KEYS, n_tok), BF16),
    ]
    return pl.pallas_call(
        kern,
        grid=(n_tok // tl,),
        in_specs=[
            pl.BlockSpec((D_MODEL, tl), lambda i: (0, i)),
            _full(wq_t.shape),
            _full(k1_big.shape),
            _full(k2.shape),
        ],
        out_specs=[
            pl.BlockSpec((N_KEYS, PEER_HEADS, tl), lambda i: (0, 0, i)),
            pl.BlockSpec((N_KEYS, PEER_HEADS, tl), lambda i: (0, 0, i)),
            pl.BlockSpec((PEER_HEADS, N_KEYS, tl), lambda i: (0, 0, i)),
            pl.BlockSpec((PEER_HEADS, N_KEYS, tl), lambda i: (0, 0, i)),
        ],
        out_shape=out_shape,
        scratch_shapes=[
            pltpu.VMEM((N_KEYS, PEER_HEADS, tl), F32),
            pltpu.VMEM((PEER_TOPK, PEER_HEADS, tl), jnp.int32),
            pltpu.VMEM((PEER_TOPK, PEER_HEADS, tl), F32),
            pltpu.VMEM((PEER_HEADS, N_KEYS, tl), F32),
            pltpu.VMEM((PEER_HEADS, N_KEYS, tl), F32),
            pltpu.VMEM((PEER_TOPK, PEER_HEADS, tl), F32),
        ],
        compiler_params=pltpu.CompilerParams(
            dimension_semantics=("arbitrary",), vmem_limit_bytes=V7X_VMEM_LIMIT_BYTES),
        name="peer_trunk_routing",
    )(h2t, wq_t, k1_big, k2)


def _experts_kernel(n_blocks,
                    h2t_ref, u_ref, vt_ref, a1_ref, n1_ref, b2_ref, r2_ref, x1_ref, ga2_ref, gfin_ref,
                    y_ref, acc_ref, w_scr):
    j = pl.program_id(1)

    @pl.when(j == 0)
    def _():
        acc_ref[...] = jnp.zeros_like(acc_ref)

    tm = h2t_ref.shape[1]
    sub_experts = EXPERT_SUB_ROWS * N_KEYS
    groups = N_KEYS // BF16_SUBLANES
    total = None
    for sb in range(EXPERT_ROWS // EXPERT_SUB_ROWS):
        e0 = sb * sub_experts
        act = _gelu(_dot(u_ref[e0:e0 + sub_experts, :], h2t_ref[...])).astype(BF16)
        for rr in range(EXPERT_SUB_ROWS):
            r = sb * EXPERT_SUB_ROWS + rr
            g = None
            for h in range(PEER_HEADS):
                n_b = jnp.broadcast_to(n1_ref[r, h:h + 1, :], (BF16_SUBLANES, tm)).astype(BF16)
                a_b = jnp.broadcast_to(a1_ref[r, h:h + 1, :], (BF16_SUBLANES, tm)).astype(BF16)
                r2h = r2_ref[h].reshape(groups, BF16_SUBLANES, tm)
                b2h = b2_ref[h].reshape(groups, BF16_SUBLANES, tm)
                t = jnp.where(r2h < n_b[None], b2h, jnp.zeros_like(b2h)) * a_b[None]
                g = t if g is None else g + t
            w_scr[sb, rr * N_KEYS:(rr + 1) * N_KEYS, :] = (
                g.reshape(N_KEYS, tm) * act[rr * N_KEYS:(rr + 1) * N_KEYS])
        part = _dot(vt_ref[:, e0:e0 + sub_experts], w_scr[sb])
        total = part if total is None else total + part
    acc_ref[...] += total

    @pl.when(j == n_blocks - 1)
    def _():
        x2 = x1_ref[...] + ga2_ref[0] * acc_ref[...].T
        ms = jnp.mean(x2 * x2, axis=-1, keepdims=True)
        y_ref[...] = x2 * lax.rsqrt(ms + EPS) * gfin_ref[...]


def _experts(h2t, u_bf, vt_bf, a1, n1, b2, r2, x1, ga2_tiles, g_final):
    n_tok = h2t.shape[1]
    tm = EXPERT_TOKENS
    eb = EXPERT_ROWS * N_KEYS
    n_blocks = N_KEYS // EXPERT_ROWS
    kern = functools.partial(_experts_kernel, n_blocks)
    return pl.pallas_call(
        kern,
        grid=(n_tok // tm, n_blocks),
        in_specs=[
            pl.BlockSpec((D_MODEL, tm), lambda i, j: (0, i)),
            pl.BlockSpec((eb, D_MODEL), lambda i, j: (j, 0)),
            pl.BlockSpec((D_MODEL, eb), lambda i, j: (0, j)),
            pl.BlockSpec((EXPERT_ROWS, PEER_HEADS, tm), lambda i, j: (j, 0, i)),
            pl.BlockSpec((EXPERT_ROWS, PEER_HEADS, tm), lambda i, j: (j, 0, i)),
            pl.BlockSpec((PEER_HEADS, N_KEYS, tm), lambda i, j: (0, 0, i)),
            pl.BlockSpec((PEER_HEADS, N_KEYS, tm), lambda i, j: (0, 0, i)),
            pl.BlockSpec((tm, D_MODEL), lambda i, j: (i, 0)),
            pl.BlockSpec((1, 1, D_MODEL), lambda i, j: (i, 0, 0)),
            pl.BlockSpec((1, D_MODEL), lambda i, j: (0, 0)),
        ],
        out_specs=pl.BlockSpec((tm, D_MODEL), lambda i, j: (i, 0)),
        out_shape=jax.ShapeDtypeStruct((n_tok, D_MODEL), F32),
        scratch_shapes=[
            pltpu.VMEM((D_MODEL, tm), F32),
            pltpu.VMEM((EXPERT_ROWS // EXPERT_SUB_ROWS, EXPERT_SUB_ROWS * N_KEYS, tm), BF16),
        ],
        compiler_params=pltpu.CompilerParams(
            dimension_semantics=("arbitrary", "arbitrary"), vmem_limit_bytes=V7X_VMEM_LIMIT_BYTES),
        name="peer_trunk_experts",
    )(h2t, u_bf, vt_bf, a1, n1, b2, r2, x1, ga2_tiles, g_final)


def _rope_tables(seq_len):
    rows = seq_len // GRID_W
    row = jnp.repeat(jnp.arange(rows, dtype=F32), GRID_W)
    col = jnp.tile(jnp.arange(GRID_W, dtype=F32), rows)
    inv = ROPE_BASE ** (-jnp.arange(ROPE_PAIRS_AXIS, dtype=F32) / ROPE_PAIRS_AXIS)
    ang = jnp.concatenate([row[:, None] * inv, col[:, None] * inv], axis=-1)
    cos, sin = jnp.cos(ang), jnp.sin(ang)
    reps = LANES // HEAD_DIM
    cos_t = jnp.tile(jnp.concatenate([cos, cos], axis=-1), (1, reps))
    sin_t = jnp.tile(jnp.concatenate([-sin, sin], axis=-1), (1, reps))
    return cos_t, sin_t


def _gate_weights(w):
    eye = jnp.eye(N_RNN_BLOCKS, dtype=w.dtype)
    full = jnp.einsum("ncd,nm->ncmd", w, eye).reshape(D_RNN, D_RNN)
    return jnp.stack([full[:RNN_HALF, :RNN_HALF], full[RNN_HALF:, RNN_HALF:]])


def _ga2_tiles(ga2_rows, tokens_per_row, n_tok):
    n_tiles = n_tok // EXPERT_TOKENS
    tile_row = (jnp.arange(n_tiles) * EXPERT_TOKENS) // tokens_per_row
    return ga2_rows[tile_row][:, None, :]


def kernel(x_prompt, x_sample, cache_k, cache_v, state_rnn, c, c_ctx, w_mod, b_mod, g_norm_mix, g_norm_ffn, w_in, conv_w, conv_b, rg_w_a, rg_b_a, rg_w_i, rg_b_i, rg_lambda, attn_sink, w_out, peer_w_query, peer_sub_keys, peer_u, peer_v, g_final):
    assert w_mod.shape[0] == 1, "the expert stage fuses the final norm of a one-layer trunk"
    n_ctx_seq, ctx_len, _ = x_prompt.shape
    n_lat_seq, lat_len, _ = x_sample.shape
    n_past = cache_k.shape[2]
    n_ctx_tok = n_ctx_seq * ctx_len
    assert CTX_ROWS % ctx_len == 0 and n_ctx_seq % (CTX_ROWS // ctx_len) == 0
    assert ctx_len % ROW_CHUNK == 0 and lat_len % ROW_CHUNK == 0 and lat_len % WINDOW == 0
    assert lat_len % EXPERT_TOKENS == 0 and n_ctx_tok % EXPERT_TOKENS == 0
    assert n_lat_seq + 1 <= MODS_PAD_ROWS

    cos_t, sin_t = _rope_tables(lat_len)
    cvec = jnp.zeros((MODS_PAD_ROWS, D_MODEL), F32).at[:n_lat_seq].set(c).at[n_lat_seq].set(c_ctx)
    mods = _mods(cvec, w_mod[0], b_mod[0])
    mods_lat = mods[:n_lat_seq].reshape(n_lat_seq, 1, N_MOD * D_MODEL)
    mods_ctx = mods[n_lat_seq:n_lat_seq + 1].reshape(1, 1, N_MOD * D_MODEL)

    half_q = PEER_DQ // 2
    w_in_b = w_in[0].astype(BF16)
    w_out_b = w_out[0].astype(BF16)
    wg = jnp.stack([_gate_weights(rg_w_a[0, 0]), _gate_weights(rg_w_i[0, 0]),
                    _gate_weights(rg_w_a[0, 1]), _gate_weights(rg_w_i[0, 1])]).astype(BF16)
    bg = jnp.stack([rg_b_a[0, 0], rg_b_i[0, 0], rg_b_a[0, 1], rg_b_i[0, 1]])
    gmix = g_norm_mix[0].reshape(1, D_MODEL)
    gffn = g_norm_ffn[0].reshape(1, D_MODEL)
    gfin = g_final.reshape(1, D_MODEL)
    cb = conv_b[0].reshape(1, D_RNN)
    sink = attn_sink[0]
    wq = peer_w_query[0].reshape(D_MODEL, PEER_HEADS, 2, half_q)
    wq_t = jnp.transpose(wq, (2, 1, 3, 0)).reshape(2 * PEER_HEADS * half_q, D_MODEL).astype(BF16)
    eye_h = jnp.eye(PEER_HEADS, dtype=F32)
    k1_big = jnp.einsum("hkq,hg->khgq", peer_sub_keys[0, :, 0], eye_h).reshape(
        N_KEYS * PEER_HEADS, PEER_HEADS * half_q).astype(BF16)
    k2 = peer_sub_keys[0, :, 1].astype(BF16)
    u_bf = peer_u[0].astype(BF16)
    vt_bf = peer_v[0].T.astype(BF16)

    h0_ctx = jnp.zeros((n_ctx_seq, 2, D_RNN), F32)
    x1_c, h2t_c, k_l, v_l, rnn_l = _ctx_mixer(
        x_prompt, mods_ctx, sink, gmix, gffn, w_in_b, conv_w[0], cb, wg, bg, rg_lambda[0], w_out_b, h0_ctx)

    ck = cache_k[:, 0].reshape(n_lat_seq, n_past, KV_W)
    cv = cache_v[:, 0].reshape(n_lat_seq, n_past, KV_W)
    x1_s, h2t_s = _lat_mixer(
        x_sample, mods_lat, sink, gmix, gffn, w_in_b, conv_w[0], cb, wg, bg, rg_lambda[0], w_out_b,
        state_rnn[:, 0], ck, cv, cos_t, sin_t)

    ga2_off = (N_MOD - 1) * D_MODEL
    outs = []
    for x1, h2t, ga2_rows, per_row in (
            (x1_c, h2t_c, mods[n_lat_seq:n_lat_seq + 1, ga2_off:], n_ctx_tok),
            (x1_s, h2t_s, mods[:n_lat_seq, ga2_off:], lat_len)):
        n_tok = h2t.shape[1]
        a1, n1, b2, r2 = _routing(h2t, wq_t, k1_big, k2)
        ga2 = _ga2_tiles(ga2_rows, per_row, n_tok)
        outs.append(_experts(h2t, u_bf, vt_bf, a1, n1, b2, r2, x1.reshape(n_tok, D_MODEL), ga2, gfin))
    y_prompt = outs[0].reshape(n_ctx_seq, ctx_len, D_MODEL)
    y_sample = outs[1].reshape(n_lat_seq, lat_len, D_MODEL)

    new_k = k_l.reshape(n_ctx_seq, 1, ctx_len, N_KV_HEADS, HEAD_DIM)
    new_v = v_l.reshape(n_ctx_seq, 1, ctx_len, N_KV_HEADS, HEAD_DIM)
    new_rnn = rnn_l[:, None]
    return (y_prompt, y_sample, new_k, new_v, new_rnn)
```

```python
import functools
import math

import jax
import jax.numpy as jnp
from jax import lax
from jax.experimental import pallas as pl
from jax.experimental.pallas import tpu as pltpu

F32 = jnp.float32
BF16 = jnp.bfloat16

D_MODEL = 1024
GRID_W = 64
N_HEADS = 8
N_KV_HEADS = 2
HEAD_DIM = 64
ATTN_W = N_HEADS * HEAD_DIM
KV_W = N_KV_HEADS * HEAD_DIM
WINDOW = 128
ATTN_SCALE = HEAD_DIM ** -0.5
ROPE_BASE = 10000.0
ROPE_PAIRS_AXIS = HEAD_DIM // 4
D_RNN = 512
N_RNN_BLOCKS = 8
CONV_W = 4
CONV_LEFT = 2
RG_C = 8.0
D_IN = ATTN_W + 2 * KV_W + 2 * D_RNN
N_MOD = 6
PEER_HEADS = 8
N_KEYS = 128
PEER_DQ = 256
PEER_TOPK = 16
EPS = 1e-6

COL_K = ATTN_W
COL_V = ATTN_W + KV_W
COL_XR = ATTN_W + 2 * KV_W
COL_YG = COL_XR + D_RNN

LANES = 128
V7X_VMEM_LIMIT_BYTES = 56 * 1024 * 1024

MODS_COLS = 512
CTX_ROWS = 512
ROW_CHUNK = 256
RNN_HALF = D_RNN // 2
ROUTE_TOKENS = 256
EXPERT_TOKENS = 512
EXPERT_ROWS = 16
EXPERT_SUB_ROWS = 4
BF16_SUBLANES = 16
MODS_PAD_ROWS = 16


def _dot(a, b):
    return jnp.dot(a, b, preferred_element_type=F32)


def _dot_nt(a, b):
    return lax.dot_general(a, b, (((1,), (1,)), ((), ())), preferred_element_type=F32)


def _gelu(x):
    return 0.5 * x * (1.0 + jnp.tanh(math.sqrt(2.0 / math.pi) * (x + 0.044715 * (x * x * x))))


def _norm_mod(x, g, shift, scale):
    ms = jnp.mean(x * x, axis=-1, keepdims=True)
    return (x * lax.rsqrt(ms + EPS) * g) * (1.0 + scale) + shift


def _softplus(z):
    return jnp.maximum(z, 0.0) + jnp.log1p(jnp.exp(-jnp.abs(z)))


def _full(shape):
    return pl.BlockSpec(shape, lambda i: (0,) * len(shape), pipeline_mode=pl.Buffered(1))


def _mods_kernel(c_ref, w_ref, b_ref, o_ref):
    c = c_ref[...]
    s = (c * jax.nn.sigmoid(c)).astype(BF16)
    o_ref[...] = _dot(s, w_ref[...].astype(BF16)) + b_ref[...]


def _mods(cvec, w_mod, b_mod):
    n_out = w_mod.shape[1]
    return pl.pallas_call(
        _mods_kernel,
        grid=(n_out // MODS_COLS,),
        in_specs=[
            pl.BlockSpec((MODS_PAD_ROWS, D_MODEL), lambda j: (0, 0)),
            pl.BlockSpec((D_MODEL, MODS_COLS), lambda j: (0, j)),
            pl.BlockSpec((1, MODS_COLS), lambda j: (0, j)),
        ],
        out_specs=pl.BlockSpec((MODS_PAD_ROWS, MODS_COLS), lambda j: (0, j)),
        out_shape=jax.ShapeDtypeStruct((MODS_PAD_ROWS, n_out), F32),
        name="peer_trunk_mods",
    )(cvec, w_mod, b_mod.reshape(1, n_out))


def _split_mods(m):
    return [m[:, i * D_MODEL:(i + 1) * D_MODEL] for i in range(N_MOD)]


def _chunks(seqs, seq_len):
    return [(r, r // seq_len, r % seq_len) for r in range(0, seqs * seq_len, ROW_CHUNK)]


def _norm_to_scratch(x_ref, seqs, seq_len, mods, gmix_ref, hb_scr):
    for r, s, o in _chunks(seqs, seq_len):
        h = _norm_mod(x_ref[s, o:o + ROW_CHUNK, :], gmix_ref[...], mods[0], mods[1])
        hb_scr[r:r + ROW_CHUNK, :] = h.astype(BF16)


def _conv(xr, cw, cb):
    n = xr.shape[0]
    row = lax.broadcasted_iota(jnp.int32, xr.shape, 0)
    y = cb
    for j in range(CONV_W):
        off = j - CONV_LEFT
        if off == 0:
            xs = xr
        elif off < 0:
            xs = jnp.where(row >= -off, pltpu.roll(xr, -off, 0), 0.0)
        else:
            xs = jnp.where(row < n - off, pltpu.roll(xr, n - off, 0), 0.0)
        y = y + xs * cw[j:j + 1, :]
    return y


def _rnn_inputs_to_scratch(hb_scr, win_ref, cw_ref, cb_ref, xc_scr, gy_scr, seqs, seq_len):
    xr = _dot(hb_scr[...], win_ref[:, COL_XR:COL_YG])
    for s in range(seqs):
        r0 = s * seq_len
        xc_scr[r0:r0 + seq_len, :] = _conv(xr[r0:r0 + seq_len], cw_ref[...], cb_ref[...])
    gy_scr[...] = _gelu(_dot(hb_scr[...], win_ref[:, COL_YG:]))


def _scan(a, b, reverse):
    n = a.shape[0]
    row = lax.broadcasted_iota(jnp.int32, a.shape, 0)
    d = 1
    while d < n:
        if reverse:
            keep = row < n - d
            shift = n - d
        else:
            keep = row >= d
            shift = d
        a_s = jnp.where(keep, pltpu.roll(a, shift, 0), 1.0)
        b_s = jnp.where(keep, pltpu.roll(b, shift, 0), 0.0)
        b = a * b_s + b
        a = a * a_s
        d *= 2
    return a, b


def _rglru_to_mix(xc_scr, gy_scr, mix_scr, wg_ref, bg_ref, lam_ref, h0, r0, seq_len, state_out):
    sp = _softplus(-lam_ref[...])
    for hf in range(D_RNN // RNN_HALF):
        cols = slice(hf * RNN_HALF, (hf + 1) * RNN_HALF)
        xc = xc_scr[r0:r0 + seq_len, cols]
        xcb = xc.astype(BF16)
        total = None
        for d in range(2):
            r = jax.nn.sigmoid(_dot(xcb, wg_ref[2 * d, hf]) + bg_ref[2 * d:2 * d + 1, cols])
            gi = jax.nn.sigmoid(_dot(xcb, wg_ref[2 * d + 1, hf]) + bg_ref[2 * d + 1:2 * d + 2, cols])
            log_a = -RG_C * r * sp[d:d + 1, cols]
            a = jnp.exp(log_a)
            th = jnp.tanh(log_a)
            b = jnp.sqrt(-2.0 * th / (1.0 - th)) * gi * xc
            a_cum, b_cum = _scan(a, b, reverse=(d == 1))
            hd = a_cum * h0[d:d + 1, cols] + b_cum
            if state_out is not None:
                state_out(d, cols, hd)
            total = hd if total is None else total + hd
        mix_scr[r0:r0 + seq_len, ATTN_W + hf * RNN_HALF:ATTN_W + (hf + 1) * RNN_HALF] = (
            total * gy_scr[r0:r0 + seq_len, cols]).astype(BF16)


def _kv_variants(x):
    lane = lax.broadcasted_iota(jnp.int32, x.shape, 1)
    low = lane < HEAD_DIM
    xr = pltpu.roll(x, HEAD_DIM, 1)
    return [[jnp.where(low, x, 0.0), jnp.where(low, 0.0, xr)],
            [jnp.where(low, xr, 0.0), jnp.where(low, 0.0, x)]]


def _bf16_variants(x):
    return [[t.astype(BF16) for t in pair] for pair in _kv_variants(x)]


def _tail(x_ref, x1_ref, h2t_ref, mix_scr, wout_ref, mods, gffn_ref, seqs, seq_len):
    _, _, ga1, sh2, sc2, _ = mods
    for r, s, o in _chunks(seqs, seq_len):
        x1 = x_ref[s, o:o + ROW_CHUNK, :] + ga1 * _dot(mix_scr[r:r + ROW_CHUNK, :], wout_ref[...])
        x1_ref[s, o:o + ROW_CHUNK, :] = x1
        h2 = _norm_mod(x1, gffn_ref[...], sh2, sc2)
        h2t_ref[:, r:r + ROW_CHUNK] = h2.T.astype(BF16)


def _ctx_mixer_kernel(seqs, seq_len,
                      sink_ref, x_ref, mod_ref, gmix_ref, gffn_ref, win_ref, cw_ref, cb_ref,
                      wg_ref, bg_ref, lam_ref, wout_ref, h0_ref,
                      x1_ref, h2t_ref, k_ref, v_ref, rnn_ref,
                      hb_scr, xc_scr, gy_scr, mix_scr):
    mods = _split_mods(mod_ref[0])
    _norm_to_scratch(x_ref, seqs, seq_len, mods, gmix_ref, hb_scr)
    q_all = _dot(hb_scr[...], win_ref[:, :COL_K]).astype(BF16)
    kv_all = _dot(hb_scr[...], win_ref[:, COL_K:COL_XR])
    _rnn_inputs_to_scratch(hb_scr, win_ref, cw_ref, cb_ref, xc_scr, gy_scr, seqs, seq_len)

    for s in range(seqs):
        r0 = s * seq_len
        k = kv_all[r0:r0 + seq_len, :KV_W]
        v = kv_all[r0:r0 + seq_len, KV_W:]
        k_ref[s] = k
        v_ref[s] = v
        kvar = _bf16_variants(k)
        vvar = _bf16_variants(v)
        for j in range(N_HEADS // 2):
            kh = (2 * j) // (N_HEADS // N_KV_HEADS)
            qp = q_all[r0:r0 + seq_len, j * LANES:(j + 1) * LANES]
            o = None
            for var in range(2):
                sink = sink_ref[2 * j + var]
                sc = _dot_nt(qp, kvar[kh][var]) * ATTN_SCALE
                m = jnp.maximum(jnp.max(sc, axis=-1, keepdims=True), sink)
                e = jnp.exp(sc - m)
                den = jnp.sum(e, axis=-1, keepdims=True) + jnp.exp(sink - m)
                pv = _dot((e / den).astype(BF16), vvar[kh][var])
                o = pv if o is None else o + pv
            mix_scr[r0:r0 + seq_len, j * LANES:(j + 1) * LANES] = o.astype(BF16)

        def state_out(d, cols, hd, s=s):
            row = seq_len - 1 if d == 0 else 0
            rnn_ref[s, d:d + 1, cols] = hd[row:row + 1, :]

        _rglru_to_mix(xc_scr, gy_scr, mix_scr, wg_ref, bg_ref, lam_ref, h0_ref[s], r0, seq_len, state_out)

    _tail(x_ref, x1_ref, h2t_ref, mix_scr, wout_ref, mods, gffn_ref, seqs, seq_len)


def _mixer_scratch(rows):
    return [
        pltpu.VMEM((rows, D_MODEL), BF16),
        pltpu.VMEM((rows, D_RNN), F32),
        pltpu.VMEM((rows, D_RNN), F32),
        pltpu.VMEM((rows, ATTN_W + D_RNN), BF16),
    ]


def _ctx_mixer(x, mods_row, sink, g_mix, g_ffn, w_in, conv_w, conv_b, wg, bg, lam, w_out, h0):
    n_seq, seq_len, _ = x.shape
    seqs = CTX_ROWS // seq_len
    rows = seqs * seq_len
    kern = functools.partial(_ctx_mixer_kernel, seqs, seq_len)
    return pl.pallas_call(
        kern,
        grid=(n_seq // seqs,),
        in_specs=[
            pl.BlockSpec(memory_space=pltpu.SMEM),
            pl.BlockSpec((seqs, seq_len, D_MODEL), lambda i: (i, 0, 0)),
            _full((1, 1, N_MOD * D_MODEL)),
            _full((1, D_MODEL)),
            _full((1, D_MODEL)),
            _full((D_MODEL, D_IN)),
            _full((CONV_W, D_RNN)),
            _full((1, D_RNN)),
            _full((4, 2, RNN_HALF, RNN_HALF)),
            _full((4, D_RNN)),
            _full((2, D_RNN)),
            _full((ATTN_W + D_RNN, D_MODEL)),
            pl.BlockSpec((seqs, 2, D_RNN), lambda i: (i, 0, 0)),
        ],
        out_specs=[
            pl.BlockSpec((seqs, seq_len, D_MODEL), lambda i: (i, 0, 0)),
            pl.BlockSpec((D_MODEL, rows), lambda i: (0, i)),
            pl.BlockSpec((seqs, seq_len, KV_W), lambda i: (i, 0, 0)),
            pl.BlockSpec((seqs, seq_len, KV_W), lambda i: (i, 0, 0)),
            pl.BlockSpec((seqs, 2, D_RNN), lambda i: (i, 0, 0)),
        ],
        out_shape=[
            jax.ShapeDtypeStruct((n_seq, seq_len, D_MODEL), F32),
            jax.ShapeDtypeStruct((D_MODEL, n_seq * seq_len), BF16),
            jax.ShapeDtypeStruct((n_seq, seq_len, KV_W), F32),
            jax.ShapeDtypeStruct((n_seq, seq_len, KV_W), F32),
            jax.ShapeDtypeStruct((n_seq, 2, D_RNN), F32),
        ],
        scratch_shapes=_mixer_scratch(rows),
        compiler_params=pltpu.CompilerParams(
            dimension_semantics=("arbitrary",), vmem_limit_bytes=V7X_VMEM_LIMIT_BYTES),
        name="peer_trunk_ctx_mixer",
    )(sink, x, mods_row, g_mix, g_ffn, w_in, conv_w, conv_b, wg, bg, lam, w_out, h0)


def _lat_mixer_kernel(seq_len,
                      sink_ref, x_ref, mod_ref, gmix_ref, gffn_ref, win_ref, cw_ref, cb_ref,
                      wg_ref, bg_ref, lam_ref, wout_ref, h0_ref, ck_ref, cv_ref, cos_ref, sin_ref,
                      x1_ref, h2t_ref,
                      hb_scr, xc_scr, gy_scr, mix_scr, q_scr, k_scr, v_scr):
    n_blk = seq_len // WINDOW
    mods = _split_mods(mod_ref[0])
    _norm_to_scratch(x_ref, 1, seq_len, mods, gmix_ref, hb_scr)

    lane = lax.broadcasted_iota(jnp.int32, (seq_len, LANES), 1)
    first_half = jnp.bitwise_and(lane, HEAD_DIM - 1) < (HEAD_DIM // 2)

    def rope(t):
        partner = jnp.where(first_half, pltpu.roll(t, LANES - HEAD_DIM // 2, 1),
                            pltpu.roll(t, HEAD_DIM // 2, 1))
        return t * cos_ref[...] + partner * sin_ref[...]

    q = _dot(hb_scr[...], win_ref[:, :COL_K])
    for j in range(N_HEADS // 2):
        q_scr[:, j * LANES:(j + 1) * LANES] = rope(q[:, j * LANES:(j + 1) * LANES]).astype(BF16)
    kv = _dot(hb_scr[...], win_ref[:, COL_K:COL_XR])
    kvar = _bf16_variants(rope(kv[:, :KV_W]))
    vvar = _bf16_variants(kv[:, KV_W:])
    zeros = jnp.zeros((WINDOW, LANES), BF16)
    for kh in range(N_KV_HEADS):
        for var in range(2):
            for scr, val in ((k_scr, kvar), (v_scr, vvar)):
                scr[kh, var, 0:WINDOW, :] = zeros
                scr[kh, var, WINDOW:WINDOW + seq_len, :] = val[kh][var]
                scr[kh, var, WINDOW + seq_len:, :] = zeros
    ckv = _bf16_variants(ck_ref[0])
    cvv = _bf16_variants(cv_ref[0])
    _rnn_inputs_to_scratch(hb_scr, win_ref, cw_ref, cb_ref, xc_scr, gy_scr, 1, seq_len)

    band = 3 * WINDOW
    qi = lax.broadcasted_iota(jnp.int32, (WINDOW, band), 0)
    kj = lax.broadcasted_iota(jnp.int32, (WINDOW, band), 1)
    in_window = (kj - qi >= 0) & (kj - qi <= 2 * WINDOW)

    def block(n, carry):
        r0 = pl.multiple_of(n * WINDOW, WINDOW)
        kpos = kj + (n - 1) * WINDOW
        mask = in_window & (kpos >= 0) & (kpos < seq_len)
        for j in range(N_HEADS // 2):
            kh = (2 * j) // (N_HEADS // N_KV_HEADS)
            qp = q_scr[pl.ds(r0, WINDOW), j * LANES:(j + 1) * LANES]
            o = None
            for var in range(2):
                sink = sink_ref[2 * j + var]
                kw = k_scr[kh, var, pl.ds(r0, band), :]
                vw = v_scr[kh, var, pl.ds(r0, band), :]
                sb = jnp.where(mask, _dot_nt(qp, kw) * ATTN_SCALE, -jnp.inf)
                sc = _dot_nt(qp, ckv[kh][var]) * ATTN_SCALE
                m = jnp.maximum(jnp.maximum(jnp.max(sb, axis=-1, keepdims=True),
                                            jnp.max(sc, axis=-1, keepdims=True)), sink)
                eb = jnp.exp(sb - m)
                ec = jnp.exp(sc - m)
                den = (jnp.sum(eb, axis=-1, keepdims=True) + jnp.sum(ec, axis=-1, keepdims=True)
                       + jnp.exp(sink - m))
                inv = 1.0 / den
                pv = _dot((eb * inv).astype(BF16), vw) + _dot((ec * inv).astype(BF16), cvv[kh][var])
                o = pv if o is None else o + pv
            mix_scr[pl.ds(r0, WINDOW), j * LANES:(j + 1) * LANES] = o.astype(BF16)
        return carry

    lax.fori_loop(0, n_blk, block, 0)

    _rglru_to_mix(xc_scr, gy_scr, mix_scr, wg_ref, bg_ref, lam_ref, h0_ref[0], 0, seq_len, None)
    _tail(x_ref, x1_ref, h2t_ref, mix_scr, wout_ref, mods, gffn_ref, 1, seq_len)


def _lat_mixer(x, mods_rows, sink, g_mix, g_ffn, w_in, conv_w, conv_b, wg, bg, lam, w_out, h0,
               cache_k, cache_v, cos_t, sin_t):
    n_seq, seq_len, _ = x.shape
    n_ctx = cache_k.shape[1]
    kern = functools.partial(_lat_mixer_kernel, seq_len)
    return pl.pallas_call(
        kern,
        grid=(n_seq,),
        in_specs=[
            pl.BlockSpec(memory_space=pltpu.SMEM),
            pl.BlockSpec((1, seq_len, D_MODEL), lambda i: (i, 0, 0)),
            pl.BlockSpec((1, 1, N_MOD * D_MODEL), lambda i: (i, 0, 0)),
            _full((1, D_MODEL)),
            _full((1, D_MODEL)),
            _full((D_MODEL, D_IN)),
            _full((CONV_W, D_RNN)),
            _full((1, D_RNN)),
            _full((4, 2, RNN_HALF, RNN_HALF)),
            _full((4, D_RNN)),
            _full((2, D_RNN)),
            _full((ATTN_W + D_RNN, D_MODEL)),
            pl.BlockSpec((1, 2, D_RNN), lambda i: (i, 0, 0)),
            pl.BlockSpec((1, n_ctx, KV_W), lambda i: (i, 0, 0)),
            pl.BlockSpec((1, n_ctx, KV_W), lambda i: (i, 0, 0)),
            _full((seq_len, LANES)),
            _full((seq_len, LANES)),
        ],
        out_specs=[
            pl.BlockSpec((1, seq_len, D_MODEL), lambda i: (i, 0, 0)),
            pl.BlockSpec((D_MODEL, seq_len), lambda i: (0, i)),
        ],
        out_shape=[
            jax.ShapeDtypeStruct((n_seq, seq_len, D_MODEL), F32),
            jax.ShapeDtypeStruct((D_MODEL, n_seq * seq_len), BF16),
        ],
        scratch_shapes=_mixer_scratch(seq_len) + [
            pltpu.VMEM((seq_len, ATTN_W), BF16),
            pltpu.VMEM((N_KV_HEADS, 2, seq_len + 2 * WINDOW, LANES), BF16),
            pltpu.VMEM((N_KV_HEADS, 2, seq_len + 2 * WINDOW, LANES), BF16),
        ],
        compiler_params=pltpu.CompilerParams(
            dimension_semantics=("arbitrary",), vmem_limit_bytes=V7X_VMEM_LIMIT_BYTES),
        name="peer_trunk_lat_mixer",
    )(sink, x, mods_rows, g_mix, g_ffn, w_in, conv_w, conv_b, wg, bg, lam, w_out, h0,
      cache_k, cache_v, cos_t, sin_t)


def _staircase():
    return [(k1, k2) for k1 in range(PEER_TOPK) for k2 in range(PEER_TOPK)
            if (k1 + 1) * (k2 + 1) <= PEER_TOPK]


def _routing_kernel(tl,
                    h2t_ref, wq_ref, k1_ref, k2_ref,
                    a1_ref, n1_ref, b2_ref, r2_ref,
                    x1_scr, i1_scr, v1_scr, x2_scr, r2_scr, v2_scr):
    n_side = PEER_HEADS * (PEER_DQ // 2)
    half_q = PEER_DQ // 2
    qt = _dot(wq_ref[...], h2t_ref[...]).astype(BF16)

    def scores1():
        return _dot(k1_ref[...], qt[:n_side]).reshape(N_KEYS, PEER_HEADS, tl)

    def scores2(h):
        return _dot(k2_ref[h], qt[n_side + h * half_q:n_side + (h + 1) * half_q])

    x1_scr[...] = scores1()
    for h in range(PEER_HEADS):
        x2_scr[h] = scores2(h)
    r2_scr[...] = jnp.full((PEER_HEADS, N_KEYS, tl), float(PEER_TOPK), F32)

    iota1 = lax.broadcasted_iota(jnp.int32, (N_KEYS, PEER_HEADS, tl), 0)
    iota2 = lax.broadcasted_iota(jnp.int32, (N_KEYS, tl), 0)
    sub = lax.broadcasted_iota(jnp.int32, (PEER_HEADS, tl), 0)

    def pick(k, carry):
        x = x1_scr[...]
        m = jnp.max(x, axis=0, keepdims=True)
        first = jnp.min(jnp.where(x == m, iota1, N_KEYS), axis=0, keepdims=True)
        x1_scr[...] = jnp.where(iota1 == first, -jnp.inf, x)
        v1_scr[k] = m[0]
        i1_scr[k] = first[0]
        kf = lax.convert_element_type(k, F32)
        v2k = jnp.zeros((PEER_HEADS, tl), F32)
        for h in range(PEER_HEADS):
            xh = x2_scr[h]
            mh = jnp.max(xh, axis=0, keepdims=True)
            fh = jnp.min(jnp.where(xh == mh, iota2, N_KEYS), axis=0, keepdims=True)
            sel = iota2 == fh
            x2_scr[h] = jnp.where(sel, -jnp.inf, xh)
            r2_scr[h] = jnp.where(sel, kf, r2_scr[h])
            v2k = jnp.where(sub == h, jnp.broadcast_to(mh, (PEER_HEADS, tl)), v2k)
        v2_scr[k] = v2k
        return carry

    lax.fori_loop(0, PEER_TOPK, pick, 0)

    v1 = [v1_scr[k] for k in range(PEER_TOPK)]
    v2 = [v2_scr[k] for k in range(PEER_TOPK)]
    cand = _staircase()
    c = [v1[k1] + v2[k2] for (k1, k2) in cand]
    zero = jnp.zeros((PEER_HEADS, tl), F32)
    rank = [zero for _ in cand]
    for i, (a1, a2) in enumerate(cand):
        for j in range(i + 1, len(cand)):
            b1, b2 = cand[j]
            if a1 <= b1 and a2 <= b2:
                rank[j] = rank[j] + 1.0
            else:
                ge = jnp.where(c[i] >= c[j], 1.0, 0.0)
                rank[j] = rank[j] + ge
                rank[i] = rank[i] + (1.0 - ge)
    chosen = [jnp.where(r < float(PEER_TOPK), 1.0, 0.0) for r in rank]

    ea = [jnp.exp(v1[k] - v1[0]) for k in range(PEER_TOPK)]
    eb = [jnp.exp(v2[k] - v2[0]) for k in range(PEER_TOPK)]
    n_row = [zero for _ in range(PEER_TOPK)]
    z = zero
    for idx, (k1, k2) in enumerate(cand):
        n_row[k1] = n_row[k1] + chosen[idx]
        z = z + chosen[idx] * (ea[k1] * eb[k2])
    inv_z = 1.0 / z

    n1 = jnp.zeros((N_KEYS, PEER_HEADS, tl), F32)
    for k in range(PEER_TOPK):
        n1 = n1 + jnp.where(iota1 == i1_scr[k][None], n_row[k][None], 0.0)
    n1_ref[...] = n1
    a1_ref[...] = jnp.exp(scores1() - v1[0][None]) * inv_z[None]
    for h in range(PEER_HEADS):
        b2_ref[h] = jnp.exp(scores2(h) - v2[0][h:h + 1, :]).astype(BF16)
    r2_ref[...] = r2_scr[...].astype(BF16)


def _routing(h2t, wq_t, k1_big, k2):
    n_tok = h2t.shape[1]
    tl = ROUTE_TOKENS
    kern = functools.partial(_routing_kernel, tl)
    out_shape = [
        jax.ShapeDtypeStruct((N_KEYS, PEER_HEADS, n_tok), F32),
        jax.ShapeDtypeStruct((N_KEYS, PEER_HEADS, n_tok), F32),
        jax.ShapeDtypeStruct((PEER_HEADS, N_KEYS, n_tok), BF16),
        jax.ShapeDtypeStruct((PEER_HEADS, N_KEYS, n_tok), BF16),
    ]
    return pl.pallas_call(
        kern,
        grid=(n_tok // tl,),
        in_specs=[
            pl.BlockSpec((D_MODEL, tl), lambda i: (0, i)),
            _full(wq_t.shape),
            _full(k1_big.shape),
            _full(k2.shape),
        ],
        out_specs=[
            pl.BlockSpec((N_KEYS, PEER_HEADS, tl), lambda i: (0, 0, i)),
            pl.BlockSpec((N_KEYS, PEER_HEADS, tl), lambda i: (0, 0, i)),
            pl.BlockSpec((PEER_HEADS, N_KEYS, tl), lambda i: (0, 0, i)),
            pl.BlockSpec((PEER_HEADS, N_KEYS, tl), lambda i: (0, 0, i)),
        ],
        out_shape=out_shape,
        scratch_shapes=[
            pltpu.VMEM((N_KEYS, PEER_HEADS, tl), F32),
            pltpu.VMEM((PEER_TOPK, PEER_HEADS, tl), jnp.int32),
            pltpu.VMEM((PEER_TOPK, PEER_HEADS, tl), F32),
            pltpu.VMEM((PEER_HEADS, N_KEYS, tl), F32),
            pltpu.VMEM((PEER_HEADS, N_KEYS, tl), F32),
            pltpu.VMEM((PEER_TOPK, PEER_HEADS, tl), F32),
        ],
        compiler_params=pltpu.CompilerParams(
            dimension_semantics=("arbitrary",), vmem_limit_bytes=V7X_VMEM_LIMIT_BYTES),
        name="peer_trunk_routing",
    )(h2t, wq_t, k1_big, k2)


def _experts_kernel(n_blocks,
                    h2t_ref, u_ref, vt_ref, a1_ref, n1_ref, b2_ref, r2_ref, x1_ref, ga2_ref, gfin_ref,
                    y_ref, acc_ref, a_scr, w_scr):
    i = pl.program_id(0)
    tm = h2t_ref.shape[1]
    sub_experts = EXPERT_SUB_ROWS * N_KEYS
    groups = N_KEYS // BF16_SUBLANES
    out_block = jnp.maximum(i - 2, 0) % n_blocks

    @pl.when(i == 0)
    def _():
        a_scr[...] = jnp.zeros_like(a_scr)
        w_scr[...] = jnp.zeros_like(w_scr)
        acc_ref[...] = jnp.zeros_like(acc_ref)

    def step(src, dst):
        total = None
        for sb in range(EXPERT_ROWS // EXPERT_SUB_ROWS):
            rows = slice(sb * sub_experts, (sb + 1) * sub_experts)
            part = _dot(vt_ref[:, rows], w_scr[src, rows, :])
            total = part if total is None else total + part

            act = _gelu(a_scr[src, rows, :]).astype(BF16)
            for rr in range(EXPERT_SUB_ROWS):
                r = sb * EXPERT_SUB_ROWS + rr
                g = None
                for h in range(PEER_HEADS):
                    n_b = jnp.broadcast_to(n1_ref[r, h:h + 1, :], (BF16_SUBLANES, tm)).astype(BF16)
                    a_b = jnp.broadcast_to(a1_ref[r, h:h + 1, :], (BF16_SUBLANES, tm)).astype(BF16)
                    r2h = r2_ref[h].reshape(groups, BF16_SUBLANES, tm)
                    b2h = b2_ref[h].reshape(groups, BF16_SUBLANES, tm)
                    t = jnp.where(r2h < n_b[None], b2h, jnp.zeros_like(b2h)) * a_b[None]
                    g = t if g is None else g + t
                w_scr[dst, sb * sub_experts + rr * N_KEYS:sb * sub_experts + (rr + 1) * N_KEYS, :] = (
                    g.reshape(N_KEYS, tm) * act[rr * N_KEYS:(rr + 1) * N_KEYS])

            a_scr[dst, rows, :] = _dot(u_ref[rows, :], h2t_ref[...])
        acc_ref[...] = jnp.where(out_block == 0, total, acc_ref[...] + total)

    @pl.when(i % 2 == 0)
    def _():
        step(0, 1)

    @pl.when(i % 2 == 1)
    def _():
        step(1, 0)

    @pl.when((i >= 2) & (out_block == n_blocks - 1))
    def _():
        x2 = x1_ref[...] + ga2_ref[0] * acc_ref[...].T
        ms = jnp.mean(x2 * x2, axis=-1, keepdims=True)
        y_ref[...] = x2 * lax.rsqrt(ms + EPS) * gfin_ref[...]


def _experts(h2t, u_bf, vt_bf, a1, n1, b2, r2, x1, ga2_tiles, g_final):
    n_tok = h2t.shape[1]
    tm = EXPERT_TOKENS
    eb = EXPERT_ROWS * N_KEYS
    n_blocks = N_KEYS // EXPERT_ROWS
    n_flat = (n_tok // tm) * n_blocks
    kern = functools.partial(_experts_kernel, n_blocks)

    def tile_block(i, lag):
        f = jnp.clip(i - lag, 0, n_flat - 1)
        return f // n_blocks, f % n_blocks

    def scores_tile(i):
        return tile_block(i, 0)[0]

    def scores_block(i):
        return tile_block(i, 0)[1]

    def gates_tile(i):
        return tile_block(i, 1)[0]

    def gates_block(i):
        return tile_block(i, 1)[1]

    def out_tile(i):
        return tile_block(i, 2)[0]

    def out_block(i):
        return tile_block(i, 2)[1]

    return pl.pallas_call(
        kern,
        grid=(n_flat + 2,),
        in_specs=[
            pl.BlockSpec((D_MODEL, tm), lambda i: (0, scores_tile(i))),
            pl.BlockSpec((eb, D_MODEL), lambda i: (scores_block(i), 0)),
            pl.BlockSpec((D_MODEL, eb), lambda i: (0, out_block(i))),
            pl.BlockSpec((EXPERT_ROWS, PEER_HEADS, tm), lambda i: (gates_block(i), 0, gates_tile(i))),
            pl.BlockSpec((EXPERT_ROWS, PEER_HEADS, tm), lambda i: (gates_block(i), 0, gates_tile(i))),
            pl.BlockSpec((PEER_HEADS, N_KEYS, tm), lambda i: (0, 0, gates_tile(i))),
            pl.BlockSpec((PEER_HEADS, N_KEYS, tm), lambda i: (0, 0, gates_tile(i))),
            pl.BlockSpec((tm, D_MODEL), lambda i: (out_tile(i), 0)),
            pl.BlockSpec((1, 1, D_MODEL), lambda i: (out_tile(i), 0, 0)),
            pl.BlockSpec((1, D_MODEL), lambda i: (0, 0)),
        ],
        out_specs=pl.BlockSpec((tm, D_MODEL), lambda i: (out_tile(i), 0)),
        out_shape=jax.ShapeDtypeStruct((n_tok, D_MODEL), F32),
        scratch_shapes=[
            pltpu.VMEM((D_MODEL, tm), F32),
            pltpu.VMEM((2, eb, tm), F32),
            pltpu.VMEM((2, eb, tm), BF16),
        ],
        compiler_params=pltpu.CompilerParams(
            dimension_semantics=("arbitrary",), vmem_limit_bytes=V7X_VMEM_LIMIT_BYTES),
        name="peer_trunk_experts",
    )(h2t, u_bf, vt_bf, a1, n1, b2, r2, x1, ga2_tiles, g_final)


def _rope_tables(seq_len):
    rows = seq_len // GRID_W
    row = jnp.repeat(jnp.arange(rows, dtype=F32), GRID_W)
    col = jnp.tile(jnp.arange(GRID_W, dtype=F32), rows)
    inv = ROPE_BASE ** (-jnp.arange(ROPE_PAIRS_AXIS, dtype=F32) / ROPE_PAIRS_AXIS)
    ang = jnp.concatenate([row[:, None] * inv, col[:, None] * inv], axis=-1)
    cos, sin = jnp.cos(ang), jnp.sin(ang)
    reps = LANES // HEAD_DIM
    cos_t = jnp.tile(jnp.concatenate([cos, cos], axis=-1), (1, reps))
    sin_t = jnp.tile(jnp.concatenate([-sin, sin], axis=-1), (1, reps))
    return cos_t, sin_t


def _gate_weights(w):
    eye = jnp.eye(N_RNN_BLOCKS, dtype=w.dtype)
    full = jnp.einsum("ncd,nm->ncmd", w, eye).reshape(D_RNN, D_RNN)
    return jnp.stack([full[:RNN_HALF, :RNN_HALF], full[RNN_HALF:, RNN_HALF:]])


def _ga2_tiles(ga2_rows, tokens_per_row, n_tok):
    n_tiles = n_tok // EXPERT_TOKENS
    tile_row = (jnp.arange(n_tiles) * EXPERT_TOKENS) // tokens_per_row
    return ga2_rows[tile_row][:, None, :]


def kernel(x_prompt, x_sample, cache_k, cache_v, state_rnn, c, c_ctx, w_mod, b_mod, g_norm_mix, g_norm_ffn, w_in, conv_w, conv_b, rg_w_a, rg_b_a, rg_w_i, rg_b_i, rg_lambda, attn_sink, w_out, peer_w_query, peer_sub_keys, peer_u, peer_v, g_final):
    assert w_mod.shape[0] == 1, "the expert stage fuses the final norm of a one-layer trunk"
    n_ctx_seq, ctx_len, _ = x_prompt.shape
    n_lat_seq, lat_len, _ = x_sample.shape
    n_past = cache_k.shape[2]
    n_ctx_tok = n_ctx_seq * ctx_len
    assert CTX_ROWS % ctx_len == 0 and n_ctx_seq % (CTX_ROWS // ctx_len) == 0
    assert ctx_len % ROW_CHUNK == 0 and lat_len % ROW_CHUNK == 0 and lat_len % WINDOW == 0
    assert lat_len % EXPERT_TOKENS == 0 and n_ctx_tok % EXPERT_TOKENS == 0
    assert n_lat_seq + 1 <= MODS_PAD_ROWS

    cos_t, sin_t = _rope_tables(lat_len)
    cvec = jnp.zeros((MODS_PAD_ROWS, D_MODEL), F32).at[:n_lat_seq].set(c).at[n_lat_seq].set(c_ctx)
    mods = _mods(cvec, w_mod[0], b_mod[0])
    mods_lat = mods[:n_lat_seq].reshape(n_lat_seq, 1, N_MOD * D_MODEL)
    mods_ctx = mods[n_lat_seq:n_lat_seq + 1].reshape(1, 1, N_MOD * D_MODEL)

    half_q = PEER_DQ // 2
    w_in_b = w_in[0].astype(BF16)
    w_out_b = w_out[0].astype(BF16)
    wg = jnp.stack([_gate_weights(rg_w_a[0, 0]), _gate_weights(rg_w_i[0, 0]),
                    _gate_weights(rg_w_a[0, 1]), _gate_weights(rg_w_i[0, 1])]).astype(BF16)
    bg = jnp.stack([rg_b_a[0, 0], rg_b_i[0, 0], rg_b_a[0, 1], rg_b_i[0, 1]])
    gmix = g_norm_mix[0].reshape(1, D_MODEL)
    gffn = g_norm_ffn[0].reshape(1, D_MODEL)
    gfin = g_final.reshape(1, D_MODEL)
    cb = conv_b[0].reshape(1, D_RNN)
    sink = attn_sink[0]
    wq = peer_w_query[0].reshape(D_MODEL, PEER_HEADS, 2, half_q)
    wq_t = jnp.transpose(wq, (2, 1, 3, 0)).reshape(2 * PEER_HEADS * half_q, D_MODEL).astype(BF16)
    eye_h = jnp.eye(PEER_HEADS, dtype=F32)
    k1_big = jnp.einsum("hkq,hg->khgq", peer_sub_keys[0, :, 0], eye_h).reshape(
        N_KEYS * PEER_HEADS, PEER_HEADS * half_q).astype(BF16)
    k2 = peer_sub_keys[0, :, 1].astype(BF16)
    u_bf = peer_u[0].astype(BF16)
    vt_bf = peer_v[0].T.astype(BF16)

    h0_ctx = jnp.zeros((n_ctx_seq, 2, D_RNN), F32)
    x1_c, h2t_c, k_l, v_l, rnn_l = _ctx_mixer(
        x_prompt, mods_ctx, sink, gmix, gffn, w_in_b, conv_w[0], cb, wg, bg, rg_lambda[0], w_out_b, h0_ctx)

    ck = cache_k[:, 0].reshape(n_lat_seq, n_past, KV_W)
    cv = cache_v[:, 0].reshape(n_lat_seq, n_past, KV_W)
    x1_s, h2t_s = _lat_mixer(
        x_sample, mods_lat, sink, gmix, gffn, w_in_b, conv_w[0], cb, wg, bg, rg_lambda[0], w_out_b,
        state_rnn[:, 0], ck, cv, cos_t, sin_t)

    ga2_off = (N_MOD - 1) * D_MODEL
    outs = []
    for x1, h2t, ga2_rows, per_row in (
            (x1_c, h2t_c, mods[n_lat_seq:n_lat_seq + 1, ga2_off:], n_ctx_tok),
            (x1_s, h2t_s, mods[:n_lat_seq, ga2_off:], lat_len)):
        n_tok = h2t.shape[1]
        a1, n1, b2, r2 = _routing(h2t, wq_t, k1_big, k2)
        ga2 = _ga2_tiles(ga2_rows, per_row, n_tok)
        outs.append(_experts(h2t, u_bf, vt_bf, a1, n1, b2, r2, x1.reshape(n_tok, D_MODEL), ga2, gfin))
    y_prompt = outs[0].reshape(n_ctx_seq, ctx_len, D_MODEL)
    y_sample = outs[1].reshape(n_lat_seq, lat_len, D_MODEL)

    new_k = k_l.reshape(n_ctx_seq, 1, ctx_len, N_KV_HEADS, HEAD_DIM)
    new_v = v_l.reshape(n_ctx_seq, 1, ctx_len, N_KV_HEADS, HEAD_DIM)
    new_rnn = rnn_l[:, None]
    return (y_prompt, y_sample, new_k, new_v, new_rnn)
```

```python
import functools
import math

import jax
import jax.numpy as jnp
from jax import lax
from jax.experimental import pallas as pl
from jax.experimental.pallas import tpu as pltpu

F32 = jnp.float32
BF16 = jnp.bfloat16

D_MODEL = 1024
GRID_W = 64
N_HEADS = 8
N_KV_HEADS = 2
HEAD_DIM = 64
ATTN_W = N_HEADS * HEAD_DIM
KV_W = N_KV_HEADS * HEAD_DIM
WINDOW = 128
ATTN_SCALE = HEAD_DIM ** -0.5
ROPE_BASE = 10000.0
ROPE_PAIRS_AXIS = HEAD_DIM // 4
D_RNN = 512
N_RNN_BLOCKS = 8
CONV_W = 4
CONV_LEFT = 2
RG_C = 8.0
D_IN = ATTN_W + 2 * KV_W + 2 * D_RNN
N_MOD = 6
PEER_HEADS = 8
N_KEYS = 128
PEER_DQ = 256
PEER_TOPK = 16
EPS = 1e-6

COL_K = ATTN_W
COL_V = ATTN_W + KV_W
COL_XR = ATTN_W + 2 * KV_W
COL_YG = COL_XR + D_RNN

LANES = 128
V7X_VMEM_LIMIT_BYTES = 56 * 1024 * 1024

MODS_COLS = 512
CTX_ROWS = 512
ROW_CHUNK = 256
RNN_HALF = D_RNN // 2
ROUTE_TOKENS = 256
EXPERT_TOKENS = 512
EXPERT_ROWS = 16
EXPERT_SUB_ROWS = 4
BF16_SUBLANES = 16
MODS_PAD_ROWS = 16


def _dot(a, b):
    return jnp.dot(a, b, preferred_element_type=F32)


def _dot_nt(a, b):
    return lax.dot_general(a, b, (((1,), (1,)), ((), ())), preferred_element_type=F32)


def _gelu(x):
    return 0.5 * x * (1.0 + jnp.tanh(math.sqrt(2.0 / math.pi) * (x + 0.044715 * (x * x * x))))


def _norm_mod(x, g, shift, scale):
    ms = jnp.mean(x * x, axis=-1, keepdims=True)
    return (x * lax.rsqrt(ms + EPS) * g) * (1.0 + scale) + shift


def _softplus(z):
    return jnp.maximum(z, 0.0) + jnp.log1p(jnp.exp(-jnp.abs(z)))


def _full(shape):
    return pl.BlockSpec(shape, lambda i: (0,) * len(shape), pipeline_mode=pl.Buffered(1))


def _mods_kernel(c_ref, w_ref, b_ref, o_ref):
    c = c_ref[...]
    s = (c * jax.nn.sigmoid(c)).astype(BF16)
    o_ref[...] = _dot(s, w_ref[...].astype(BF16)) + b_ref[...]


def _mods(cvec, w_mod, b_mod):
    n_out = w_mod.shape[1]
    return pl.pallas_call(
        _mods_kernel,
        grid=(n_out // MODS_COLS,),
        in_specs=[
            pl.BlockSpec((MODS_PAD_ROWS, D_MODEL), lambda j: (0, 0)),
            pl.BlockSpec((D_MODEL, MODS_COLS), lambda j: (0, j)),
            pl.BlockSpec((1, MODS_COLS), lambda j: (0, j)),
        ],
        out_specs=pl.BlockSpec((MODS_PAD_ROWS, MODS_COLS), lambda j: (0, j)),
        out_shape=jax.ShapeDtypeStruct((MODS_PAD_ROWS, n_out), F32),
        name="peer_trunk_mods",
    )(cvec, w_mod, b_mod.reshape(1, n_out))


def _split_mods(m):
    return [m[:, i * D_MODEL:(i + 1) * D_MODEL] for i in range(N_MOD)]


def _chunks(seqs, seq_len):
    return [(r, r // seq_len, r % seq_len) for r in range(0, seqs * seq_len, ROW_CHUNK)]


def _norm_to_scratch(x_ref, seqs, seq_len, mods, gmix_ref, hb_scr):
    for r, s, o in _chunks(seqs, seq_len):
        h = _norm_mod(x_ref[s, o:o + ROW_CHUNK, :], gmix_ref[...], mods[0], mods[1])
        hb_scr[r:r + ROW_CHUNK, :] = h.astype(BF16)


def _conv(xr, cw, cb):
    n = xr.shape[0]
    row = lax.broadcasted_iota(jnp.int32, xr.shape, 0)
    y = cb
    for j in range(CONV_W):
        off = j - CONV_LEFT
        if off == 0:
            xs = xr
        elif off < 0:
            xs = jnp.where(row >= -off, pltpu.roll(xr, -off, 0), 0.0)
        else:
            xs = jnp.where(row < n - off, pltpu.roll(xr, n - off, 0), 0.0)
        y = y + xs * cw[j:j + 1, :]
    return y


def _rnn_inputs_to_scratch(hb_scr, win_ref, cw_ref, cb_ref, xc_scr, gy_scr, seqs, seq_len):
    xr = _dot(hb_scr[...], win_ref[:, COL_XR:COL_YG])
    for s in range(seqs):
        r0 = s * seq_len
        xc_scr[r0:r0 + seq_len, :] = _conv(xr[r0:r0 + seq_len], cw_ref[...], cb_ref[...])
    gy_scr[...] = _gelu(_dot(hb_scr[...], win_ref[:, COL_YG:]))


def _scan(a, b, reverse):
    n = a.shape[0]
    row = lax.broadcasted_iota(jnp.int32, a.shape, 0)
    d = 1
    while d < n:
        if reverse:
            keep = row < n - d
            shift = n - d
        else:
            keep = row >= d
            shift = d
        a_s = jnp.where(keep, pltpu.roll(a, shift, 0), 1.0)
        b_s = jnp.where(keep, pltpu.roll(b, shift, 0), 0.0)
        b = a * b_s + b
        a = a * a_s
        d *= 2
    return a, b


def _rglru_to_mix(xc_scr, gy_scr, mix_scr, wg_ref, bg_ref, lam_ref, h0, r0, seq_len, state_out):
    sp = _softplus(-lam_ref[...])
    for hf in range(D_RNN // RNN_HALF):
        cols = slice(hf * RNN_HALF, (hf + 1) * RNN_HALF)
        xc = xc_scr[r0:r0 + seq_len, cols]
        xcb = xc.astype(BF16)
        total = None
        for d in range(2):
            r = jax.nn.sigmoid(_dot(xcb, wg_ref[2 * d, hf]) + bg_ref[2 * d:2 * d + 1, cols])
            gi = jax.nn.sigmoid(_dot(xcb, wg_ref[2 * d + 1, hf]) + bg_ref[2 * d + 1:2 * d + 2, cols])
            log_a = -RG_C * r * sp[d:d + 1, cols]
            a = jnp.exp(log_a)
            th = jnp.tanh(log_a)
            b = jnp.sqrt(-2.0 * th / (1.0 - th)) * gi * xc
            a_cum, b_cum = _scan(a, b, reverse=(d == 1))
            hd = a_cum * h0[d:d + 1, cols] + b_cum
            if state_out is not None:
                state_out(d, cols, hd)
            total = hd if total is None else total + hd
        mix_scr[r0:r0 + seq_len, ATTN_W + hf * RNN_HALF:ATTN_W + (hf + 1) * RNN_HALF] = (
            total * gy_scr[r0:r0 + seq_len, cols]).astype(BF16)


def _kv_variants(x):
    lane = lax.broadcasted_iota(jnp.int32, x.shape, 1)
    low = lane < HEAD_DIM
    xr = pltpu.roll(x, HEAD_DIM, 1)
    return [[jnp.where(low, x, 0.0), jnp.where(low, 0.0, xr)],
            [jnp.where(low, xr, 0.0), jnp.where(low, 0.0, x)]]


def _bf16_variants(x):
    return [[t.astype(BF16) for t in pair] for pair in _kv_variants(x)]


def _tail(x_ref, x1_ref, h2t_ref, mix_scr, wout_ref, mods, gffn_ref, seqs, seq_len):
    _, _, ga1, sh2, sc2, _ = mods
    for r, s, o in _chunks(seqs, seq_len):
        x1 = x_ref[s, o:o + ROW_CHUNK, :] + ga1 * _dot(mix_scr[r:r + ROW_CHUNK, :], wout_ref[...])
        x1_ref[s, o:o + ROW_CHUNK, :] = x1
        h2 = _norm_mod(x1, gffn_ref[...], sh2, sc2)
        h2t_ref[:, r:r + ROW_CHUNK] = h2.T.astype(BF16)


def _ctx_mixer_kernel(seqs, seq_len,
                      sink_ref, x_ref, mod_ref, gmix_ref, gffn_ref, win_ref, cw_ref, cb_ref,
                      wg_ref, bg_ref, lam_ref, wout_ref, h0_ref,
                      x1_ref, h2t_ref, k_ref, v_ref, rnn_ref,
                      hb_scr, xc_scr, gy_scr, mix_scr):
    mods = _split_mods(mod_ref[0])
    _norm_to_scratch(x_ref, seqs, seq_len, mods, gmix_ref, hb_scr)
    q_all = _dot(hb_scr[...], win_ref[:, :COL_K]).astype(BF16)
    kv_all = _dot(hb_scr[...], win_ref[:, COL_K:COL_XR])
    _rnn_inputs_to_scratch(hb_scr, win_ref, cw_ref, cb_ref, xc_scr, gy_scr, seqs, seq_len)

    for s in range(seqs):
        r0 = s * seq_len
        k = kv_all[r0:r0 + seq_len, :KV_W]
        v = kv_all[r0:r0 + seq_len, KV_W:]
        k_ref[s] = k
        v_ref[s] = v
        kvar = _bf16_variants(k)
        vvar = _bf16_variants(v)
        for j in range(N_HEADS // 2):
            kh = (2 * j) // (N_HEADS // N_KV_HEADS)
            qp = q_all[r0:r0 + seq_len, j * LANES:(j + 1) * LANES]
            o = None
            for var in range(2):
                sink = sink_ref[2 * j + var]
                sc = _dot_nt(qp, kvar[kh][var]) * ATTN_SCALE
                m = jnp.maximum(jnp.max(sc, axis=-1, keepdims=True), sink)
                e = jnp.exp(sc - m)
                den = jnp.sum(e, axis=-1, keepdims=True) + jnp.exp(sink - m)
                pv = _dot((e / den).astype(BF16), vvar[kh][var])
                o = pv if o is None else o + pv
            mix_scr[r0:r0 + seq_len, j * LANES:(j + 1) * LANES] = o.astype(BF16)

        def state_out(d, cols, hd, s=s):
            row = seq_len - 1 if d == 0 else 0
            rnn_ref[s, d:d + 1, cols] = hd[row:row + 1, :]

        _rglru_to_mix(xc_scr, gy_scr, mix_scr, wg_ref, bg_ref, lam_ref, h0_ref[s], r0, seq_len, state_out)

    _tail(x_ref, x1_ref, h2t_ref, mix_scr, wout_ref, mods, gffn_ref, seqs, seq_len)


def _mixer_scratch(rows):
    return [
        pltpu.VMEM((rows, D_MODEL), BF16),
        pltpu.VMEM((rows, D_RNN), F32),
        pltpu.VMEM((rows, D_RNN), F32),
        pltpu.VMEM((rows, ATTN_W + D_RNN), BF16),
    ]


def _ctx_mixer(x, mods_row, sink, g_mix, g_ffn, w_in, conv_w, conv_b, wg, bg, lam, w_out, h0):
    n_seq, seq_len, _ = x.shape
    seqs = CTX_ROWS // seq_len
    rows = seqs * seq_len
    kern = functools.partial(_ctx_mixer_kernel, seqs, seq_len)
    return pl.pallas_call(
        kern,
        grid=(n_seq // seqs,),
        in_specs=[
            pl.BlockSpec(memory_space=pltpu.SMEM),
            pl.BlockSpec((seqs, seq_len, D_MODEL), lambda i: (i, 0, 0)),
            _full((1, 1, N_MOD * D_MODEL)),
            _full((1, D_MODEL)),
            _full((1, D_MODEL)),
            _full((D_MODEL, D_IN)),
            _full((CONV_W, D_RNN)),
            _full((1, D_RNN)),
            _full((4, 2, RNN_HALF, RNN_HALF)),
            _full((4, D_RNN)),
            _full((2, D_RNN)),
            _full((ATTN_W + D_RNN, D_MODEL)),
            pl.BlockSpec((seqs, 2, D_RNN), lambda i: (i, 0, 0)),
        ],
        out_specs=[
            pl.BlockSpec((seqs, seq_len, D_MODEL), lambda i: (i, 0, 0)),
            pl.BlockSpec((D_MODEL, rows), lambda i: (0, i)),
            pl.BlockSpec((seqs, seq_len, KV_W), lambda i: (i, 0, 0)),
            pl.BlockSpec((seqs, seq_len, KV_W), lambda i: (i, 0, 0)),
            pl.BlockSpec((seqs, 2, D_RNN), lambda i: (i, 0, 0)),
        ],
        out_shape=[
            jax.ShapeDtypeStruct((n_seq, seq_len, D_MODEL), F32),
            jax.ShapeDtypeStruct((D_MODEL, n_seq * seq_len), BF16),
            jax.ShapeDtypeStruct((n_seq, seq_len, KV_W), F32),
            jax.ShapeDtypeStruct((n_seq, seq_len, KV_W), F32),
            jax.ShapeDtypeStruct((n_seq, 2, D_RNN), F32),
        ],
        scratch_shapes=_mixer_scratch(rows),
        compiler_params=pltpu.CompilerParams(
            dimension_semantics=("arbitrary",), vmem_limit_bytes=V7X_VMEM_LIMIT_BYTES),
        name="peer_trunk_ctx_mixer",
    )(sink, x, mods_row, g_mix, g_ffn, w_in, conv_w, conv_b, wg, bg, lam, w_out, h0)


def _lat_mixer_kernel(seq_len,
                      sink_ref, x_ref, mod_ref, gmix_ref, gffn_ref, win_ref, cw_ref, cb_ref,
                      wg_ref, bg_ref, lam_ref, wout_ref, h0_ref, ck_ref, cv_ref, cos_ref, sin_ref,
                      x1_ref, h2t_ref,
                      hb_scr, xc_scr, gy_scr, mix_scr, q_scr, k_scr, v_scr):
    n_blk = seq_len // WINDOW
    mods = _split_mods(mod_ref[0])
    _norm_to_scratch(x_ref, 1, seq_len, mods, gmix_ref, hb_scr)

    lane = lax.broadcasted_iota(jnp.int32, (seq_len, LANES), 1)
    first_half = jnp.bitwise_and(lane, HEAD_DIM - 1) < (HEAD_DIM // 2)

    def rope(t):
        partner = jnp.where(first_half, pltpu.roll(t, LANES - HEAD_DIM // 2, 1),
                            pltpu.roll(t, HEAD_DIM // 2, 1))
        return t * cos_ref[...] + partner * sin_ref[...]

    q = _dot(hb_scr[...], win_ref[:, :COL_K])
    for j in range(N_HEADS // 2):
        q_scr[:, j * LANES:(j + 1) * LANES] = rope(q[:, j * LANES:(j + 1) * LANES]).astype(BF16)
    kv = _dot(hb_scr[...], win_ref[:, COL_K:COL_XR])
    kvar = _bf16_variants(rope(kv[:, :KV_W]))
    vvar = _bf16_variants(kv[:, KV_W:])
    zeros = jnp.zeros((WINDOW, LANES), BF16)
    for kh in range(N_KV_HEADS):
        for var in range(2):
            for scr, val in ((k_scr, kvar), (v_scr, vvar)):
                scr[kh, var, 0:WINDOW, :] = zeros
                scr[kh, var, WINDOW:WINDOW + seq_len, :] = val[kh][var]
                scr[kh, var, WINDOW + seq_len:, :] = zeros
    ckv = _bf16_variants(ck_ref[0])
    cvv = _bf16_variants(cv_ref[0])
    _rnn_inputs_to_scratch(hb_scr, win_ref, cw_ref, cb_ref, xc_scr, gy_scr, 1, seq_len)

    band = 3 * WINDOW
    qi = lax.broadcasted_iota(jnp.int32, (WINDOW, band), 0)
    kj = lax.broadcasted_iota(jnp.int32, (WINDOW, band), 1)
    in_window = (kj - qi >= 0) & (kj - qi <= 2 * WINDOW)

    def block(n, carry):
        r0 = pl.multiple_of(n * WINDOW, WINDOW)
        kpos = kj + (n - 1) * WINDOW
        mask = in_window & (kpos >= 0) & (kpos < seq_len)
        for j in range(N_HEADS // 2):
            kh = (2 * j) // (N_HEADS // N_KV_HEADS)
            qp = q_scr[pl.ds(r0, WINDOW), j * LANES:(j + 1) * LANES]
            o = None
            for var in range(2):
                sink = sink_ref[2 * j + var]
                kw = k_scr[kh, var, pl.ds(r0, band), :]
                vw = v_scr[kh, var, pl.ds(r0, band), :]
                sb = jnp.where(mask, _dot_nt(qp, kw) * ATTN_SCALE, -jnp.inf)
                sc = _dot_nt(qp, ckv[kh][var]) * ATTN_SCALE
                m = jnp.maximum(jnp.maximum(jnp.max(sb, axis=-1, keepdims=True),
                                            jnp.max(sc, axis=-1, keepdims=True)), sink)
                eb = jnp.exp(sb - m)
                ec = jnp.exp(sc - m)
                den = (jnp.sum(eb, axis=-1, keepdims=True) + jnp.sum(ec, axis=-1, keepdims=True)
                       + jnp.exp(sink - m))
                inv = 1.0 / den
                pv = _dot((eb * inv).astype(BF16), vw) + _dot((ec * inv).astype(BF16), cvv[kh][var])
                o = pv if o is None else o + pv
            mix_scr[pl.ds(r0, WINDOW), j * LANES:(j + 1) * LANES] = o.astype(BF16)
        return carry

    lax.fori_loop(0, n_blk, block, 0, unroll=2)

    _rglru_to_mix(xc_scr, gy_scr, mix_scr, wg_ref, bg_ref, lam_ref, h0_ref[0], 0, seq_len, None)
    _tail(x_ref, x1_ref, h2t_ref, mix_scr, wout_ref, mods, gffn_ref, 1, seq_len)


def _lat_mixer(x, mods_rows, sink, g_mix, g_ffn, w_in, conv_w, conv_b, wg, bg, lam, w_out, h0,
               cache_k, cache_v, cos_t, sin_t):
    n_seq, seq_len, _ = x.shape
    n_ctx = cache_k.shape[1]
    kern = functools.partial(_lat_mixer_kernel, seq_len)
    return pl.pallas_call(
        kern,
        grid=(n_seq,),
        in_specs=[
            pl.BlockSpec(memory_space=pltpu.SMEM),
            pl.BlockSpec((1, seq_len, D_MODEL), lambda i: (i, 0, 0)),
            pl.BlockSpec((1, 1, N_MOD * D_MODEL), lambda i: (i, 0, 0)),
            _full((1, D_MODEL)),
            _full((1, D_MODEL)),
            _full((D_MODEL, D_IN)),
            _full((CONV_W, D_RNN)),
            _full((1, D_RNN)),
            _full((4, 2, RNN_HALF, RNN_HALF)),
            _full((4, D_RNN)),
            _full((2, D_RNN)),
            _full((ATTN_W + D_RNN, D_MODEL)),
            pl.BlockSpec((1, 2, D_RNN), lambda i: (i, 0, 0)),
            pl.BlockSpec((1, n_ctx, KV_W), lambda i: (i, 0, 0)),
            pl.BlockSpec((1, n_ctx, KV_W), lambda i: (i, 0, 0)),
            _full((seq_len, LANES)),
            _full((seq_len, LANES)),
        ],
        out_specs=[
            pl.BlockSpec((1, seq_len, D_MODEL), lambda i: (i, 0, 0)),
            pl.BlockSpec((D_MODEL, seq_len), lambda i: (0, i)),
        ],
        out_shape=[
            jax.ShapeDtypeStruct((n_seq, seq_len, D_MODEL), F32),
            jax.ShapeDtypeStruct((D_MODEL, n_seq * seq_len), BF16),
        ],
        scratch_shapes=_mixer_scratch(seq_len) + [
            pltpu.VMEM((seq_len, ATTN_W), BF16),
            pltpu.VMEM((N_KV_HEADS, 2, seq_len + 2 * WINDOW, LANES), BF16),
            pltpu.VMEM((N_KV_HEADS, 2, seq_len + 2 * WINDOW, LANES), BF16),
        ],
        compiler_params=pltpu.CompilerParams(
            dimension_semantics=("arbitrary",), vmem_limit_bytes=V7X_VMEM_LIMIT_BYTES),
        name="peer_trunk_lat_mixer",
    )(sink, x, mods_rows, g_mix, g_ffn, w_in, conv_w, conv_b, wg, bg, lam, w_out, h0,
      cache_k, cache_v, cos_t, sin_t)


def _staircase():
    return [(k1, k2) for k1 in range(PEER_TOPK) for k2 in range(PEER_TOPK)
            if (k1 + 1) * (k2 + 1) <= PEER_TOPK]


def _routing_kernel(tl,
                    h2t_ref, wq_ref, k1_ref, k2_ref,
                    a1_ref, n1_ref, b2_ref, r2_ref,
                    x1_scr, r1_scr, v1_scr, x2_scr, r2_scr, v2_scr):
    n_side = PEER_HEADS * (PEER_DQ // 2)
    half_q = PEER_DQ // 2
    qt = _dot(wq_ref[...], h2t_ref[...]).astype(BF16)

    def scores1():
        return _dot(k1_ref[...], qt[:n_side]).reshape(N_KEYS, PEER_HEADS, tl)

    def scores2(h):
        return _dot(k2_ref[h], qt[n_side + h * half_q:n_side + (h + 1) * half_q])

    iota1 = lax.broadcasted_iota(jnp.int32, (N_KEYS, PEER_HEADS, tl), 0)
    iota2 = lax.broadcasted_iota(jnp.int32, (N_KEYS, tl), 0)
    sub = lax.broadcasted_iota(jnp.int32, (PEER_HEADS, tl), 0)
    unranked = float(PEER_TOPK)

    def reset():
        x1_scr[...] = scores1()
        r1_scr[...] = jnp.full((N_KEYS, PEER_HEADS, tl), unranked, F32)
        for h in range(PEER_HEADS):
            x2_scr[h] = scores2(h)
        r2_scr[...] = jnp.full((PEER_HEADS, N_KEYS, tl), unranked, F32)

    def top_k_rounds(break_ties):
        def pick(k, carry):
            kf = lax.convert_element_type(k, F32)
            x = x1_scr[...]
            m = jnp.max(x, axis=0, keepdims=True)
            sel = x == m
            if break_ties:
                sel = iota1 == jnp.min(jnp.where(sel, iota1, N_KEYS), axis=0, keepdims=True)
            x1_scr[...] = jnp.where(sel, -jnp.inf, x)
            r1_scr[...] = jnp.where(sel, kf, r1_scr[...])
            v1_scr[k] = m[0]
            v2k = jnp.zeros((PEER_HEADS, tl), F32)
            for h in range(PEER_HEADS):
                xh = x2_scr[h]
                mh = jnp.max(xh, axis=0, keepdims=True)
                sel = xh == mh
                if break_ties:
                    sel = iota2 == jnp.min(jnp.where(sel, iota2, N_KEYS), axis=0, keepdims=True)
                x2_scr[h] = jnp.where(sel, -jnp.inf, xh)
                r2_scr[h] = jnp.where(sel, kf, r2_scr[h])
                v2k = jnp.where(sub == h, jnp.broadcast_to(mh, (PEER_HEADS, tl)), v2k)
            v2_scr[k] = v2k
            return carry

        lax.fori_loop(0, PEER_TOPK, pick, 0)

    reset()
    top_k_rounds(break_ties=False)
    ranked = jnp.sum(jnp.where(r1_scr[...] < unranked, 1.0, 0.0), axis=0)
    excess = jnp.max(ranked, axis=(0, 1), keepdims=True)
    for h in range(PEER_HEADS):
        ranked = jnp.sum(jnp.where(r2_scr[h] < unranked, 1.0, 0.0), axis=0, keepdims=True)
        excess = jnp.maximum(excess, jnp.max(ranked, axis=(0, 1), keepdims=True))

    @pl.when(excess[0, 0] > unranked)
    def _():
        reset()
        top_k_rounds(break_ties=True)

    v1 = [v1_scr[k] for k in range(PEER_TOPK)]
    v2 = [v2_scr[k] for k in range(PEER_TOPK)]
    cand = _staircase()
    c = [v1[k1] + v2[k2] for (k1, k2) in cand]
    zero = jnp.zeros((PEER_HEADS, tl), F32)
    rank = [zero for _ in cand]
    for i, (a1, a2) in enumerate(cand):
        for j in range(i + 1, len(cand)):
            b1, b2 = cand[j]
            if a1 <= b1 and a2 <= b2:
                rank[j] = rank[j] + 1.0
            else:
                ge = jnp.where(c[i] >= c[j], 1.0, 0.0)
                rank[j] = rank[j] + ge
                rank[i] = rank[i] + (1.0 - ge)
    chosen = [jnp.where(r < float(PEER_TOPK), 1.0, 0.0) for r in rank]

    ea = [jnp.exp(v1[k] - v1[0]) for k in range(PEER_TOPK)]
    eb = [jnp.exp(v2[k] - v2[0]) for k in range(PEER_TOPK)]
    n_row = [zero for _ in range(PEER_TOPK)]
    z = zero
    for idx, (k1, k2) in enumerate(cand):
        n_row[k1] = n_row[k1] + chosen[idx]
        z = z + chosen[idx] * (ea[k1] * eb[k2])
    inv_z = 1.0 / z

    r1 = r1_scr[...]
    n1 = jnp.zeros((N_KEYS, PEER_HEADS, tl), F32)
    for k in range(PEER_TOPK):
        n1 = n1 + jnp.where(r1 == float(k), n_row[k][None], 0.0)
    n1_ref[...] = n1
    a1_ref[...] = jnp.exp(scores1() - v1[0][None]) * inv_z[None]
    for h in range(PEER_HEADS):
        b2_ref[h] = jnp.exp(scores2(h) - v2[0][h:h + 1, :]).astype(BF16)
    r2_ref[...] = r2_scr[...].astype(BF16)


def _routing(h2t, wq_t, k1_big, k2):
    n_tok = h2t.shape[1]
    tl = ROUTE_TOKENS
    kern = functools.partial(_routing_kernel, tl)
    out_shape = [
        jax.ShapeDtypeStruct((N_KEYS, PEER_HEADS, n_tok), F32),
        jax.ShapeDtypeStruct((N_KEYS, PEER_HEADS, n_tok), F32),
        jax.ShapeDtypeStruct((PEER_HEADS, N_KEYS, n_tok), BF16),
        jax.ShapeDtypeStruct((PEER_HEADS, N_KEYS, n_tok), BF16),
    ]
    return pl.pallas_call(
        kern,
        grid=(n_tok // tl,),
        in_specs=[
            pl.BlockSpec((D_MODEL, tl), lambda i: (0, i)),
            _full(wq_t.shape),
            _full(k1_big.shape),
            _full(k2.shape),
        ],
        out_specs=[
            pl.BlockSpec((N_KEYS, PEER_HEADS, tl), lambda i: (0, 0, i)),
            pl.BlockSpec((N_KEYS, PEER_HEADS, tl), lambda i: (0, 0, i)),
            pl.BlockSpec((PEER_HEADS, N_KEYS, tl), lambda i: (0, 0, i)),
            pl.BlockSpec((PEER_HEADS, N_KEYS, tl), lambda i: (0, 0, i)),
        ],
        out_shape=out_shape,
        scratch_shapes=[
            pltpu.VMEM((N_KEYS, PEER_HEADS, tl), F32),
            pltpu.VMEM((N_KEYS, PEER_HEADS, tl), F32),
            pltpu.VMEM((PEER_TOPK, PEER_HEADS, tl), F32),
            pltpu.VMEM((PEER_HEADS, N_KEYS, tl), F32),
            pltpu.VMEM((PEER_HEADS, N_KEYS, tl), F32),
            pltpu.VMEM((PEER_TOPK, PEER_HEADS, tl), F32),
        ],
        compiler_params=pltpu.CompilerParams(
            dimension_semantics=("arbitrary",), vmem_limit_bytes=V7X_VMEM_LIMIT_BYTES),
        name="peer_trunk_routing",
    )(h2t, wq_t, k1_big, k2)


def _experts_kernel(n_blocks,
                    h2t_ref, u_ref, vt_ref, a1_ref, n1_ref, b2_ref, r2_ref, x1_ref, ga2_ref, gfin_ref,
                    y_ref, acc_ref, w_scr):
    j = pl.program_id(1)

    @pl.when(j == 0)
    def _():
        acc_ref[...] = jnp.zeros_like(acc_ref)

    tm = h2t_ref.shape[1]
    sub_experts = EXPERT_SUB_ROWS * N_KEYS
    groups = N_KEYS // BF16_SUBLANES
    total = None
    for sb in range(EXPERT_ROWS // EXPERT_SUB_ROWS):
        e0 = sb * sub_experts
        act = _gelu(_dot(u_ref[e0:e0 + sub_experts, :], h2t_ref[...])).astype(BF16)
        for rr in range(EXPERT_SUB_ROWS):
            r = sb * EXPERT_SUB_ROWS + rr
            g = None
            for h in range(PEER_HEADS):
                n_b = jnp.broadcast_to(n1_ref[r, h:h + 1, :], (BF16_SUBLANES, tm)).astype(BF16)
                a_b = jnp.broadcast_to(a1_ref[r, h:h + 1, :], (BF16_SUBLANES, tm)).astype(BF16)
                r2h = r2_ref[h].reshape(groups, BF16_SUBLANES, tm)
                b2h = b2_ref[h].reshape(groups, BF16_SUBLANES, tm)
                t = jnp.where(r2h < n_b[None], b2h, jnp.zeros_like(b2h)) * a_b[None]
                g = t if g is None else g + t
            w_scr[sb, rr * N_KEYS:(rr + 1) * N_KEYS, :] = (
                g.reshape(N_KEYS, tm) * act[rr * N_KEYS:(rr + 1) * N_KEYS])
        part = _dot(vt_ref[:, e0:e0 + sub_experts], w_scr[sb])
        total = part if total is None else total + part
    acc_ref[...] += total

    @pl.when(j == n_blocks - 1)
    def _():
        x2 = x1_ref[...] + ga2_ref[0] * acc_ref[...].T
        ms = jnp.mean(x2 * x2, axis=-1, keepdims=True)
        y_ref[...] = x2 * lax.rsqrt(ms + EPS) * gfin_ref[...]


def _experts(h2t, u_bf, vt_bf, a1, n1, b2, r2, x1, ga2_tiles, g_final):
    n_tok = h2t.shape[1]
    tm = EXPERT_TOKENS
    eb = EXPERT_ROWS * N_KEYS
    n_blocks = N_KEYS // EXPERT_ROWS
    kern = functools.partial(_experts_kernel, n_blocks)
    return pl.pallas_call(
        kern,
        grid=(n_tok // tm, n_blocks),
        in_specs=[
            pl.BlockSpec((D_MODEL, tm), lambda i, j: (0, i)),
            pl.BlockSpec((eb, D_MODEL), lambda i, j: (j, 0)),
            pl.BlockSpec((D_MODEL, eb), lambda i, j: (0, j)),
            pl.BlockSpec((EXPERT_ROWS, PEER_HEADS, tm), lambda i, j: (j, 0, i)),
            pl.BlockSpec((EXPERT_ROWS, PEER_HEADS, tm), lambda i, j: (j, 0, i)),
            pl.BlockSpec((PEER_HEADS, N_KEYS, tm), lambda i, j: (0, 0, i)),
            pl.BlockSpec((PEER_HEADS, N_KEYS, tm), lambda i, j: (0, 0, i)),
            pl.BlockSpec((tm, D_MODEL), lambda i, j: (i, 0)),
            pl.BlockSpec((1, 1, D_MODEL), lambda i, j: (i, 0, 0)),
            pl.BlockSpec((1, D_MODEL), lambda i, j: (0, 0)),
        ],
        out_specs=pl.BlockSpec((tm, D_MODEL), lambda i, j: (i, 0)),
        out_shape=jax.ShapeDtypeStruct((n_tok, D_MODEL), F32),
        scratch_shapes=[
            pltpu.VMEM((D_MODEL, tm), F32),
            pltpu.VMEM((EXPERT_ROWS // EXPERT_SUB_ROWS, EXPERT_SUB_ROWS * N_KEYS, tm), BF16),
        ],
        compiler_params=pltpu.CompilerParams(
            dimension_semantics=("arbitrary", "arbitrary"), vmem_limit_bytes=V7X_VMEM_LIMIT_BYTES),
        name="peer_trunk_experts",
    )(h2t, u_bf, vt_bf, a1, n1, b2, r2, x1, ga2_tiles, g_final)


def _rope_tables(seq_len):
    rows = seq_len // GRID_W
    row = jnp.repeat(jnp.arange(rows, dtype=F32), GRID_W)
    col = jnp.tile(jnp.arange(GRID_W, dtype=F32), rows)
    inv = ROPE_BASE ** (-jnp.arange(ROPE_PAIRS_AXIS, dtype=F32) / ROPE_PAIRS_AXIS)
    ang = jnp.concatenate([row[:, None] * inv, col[:, None] * inv], axis=-1)
    cos, sin = jnp.cos(ang), jnp.sin(ang)
    reps = LANES // HEAD_DIM
    cos_t = jnp.tile(jnp.concatenate([cos, cos], axis=-1), (1, reps))
    sin_t = jnp.tile(jnp.concatenate([-sin, sin], axis=-1), (1, reps))
    return cos_t, sin_t


def _gate_weights(w):
    eye = jnp.eye(N_RNN_BLOCKS, dtype=w.dtype)
    full = jnp.einsum("ncd,nm->ncmd", w, eye).reshape(D_RNN, D_RNN)
    return jnp.stack([full[:RNN_HALF, :RNN_HALF], full[RNN_HALF:, RNN_HALF:]])


def _ga2_tiles(ga2_rows, tokens_per_row, n_tok):
    n_tiles = n_tok // EXPERT_TOKENS
    tile_row = (jnp.arange(n_tiles) * EXPERT_TOKENS) // tokens_per_row
    return ga2_rows[tile_row][:, None, :]


def kernel(x_prompt, x_sample, cache_k, cache_v, state_rnn, c, c_ctx, w_mod, b_mod, g_norm_mix, g_norm_ffn, w_in, conv_w, conv_b, rg_w_a, rg_b_a, rg_w_i, rg_b_i, rg_lambda, attn_sink, w_out, peer_w_query, peer_sub_keys, peer_u, peer_v, g_final):
    assert w_mod.shape[0] == 1, "the expert stage fuses the final norm of a one-layer trunk"
    n_ctx_seq, ctx_len, _ = x_prompt.shape
    n_lat_seq, lat_len, _ = x_sample.shape
    n_past = cache_k.shape[2]
    n_ctx_tok = n_ctx_seq * ctx_len
    assert CTX_ROWS % ctx_len == 0 and n_ctx_seq % (CTX_ROWS // ctx_len) == 0
    assert ctx_len % ROW_CHUNK == 0 and lat_len % ROW_CHUNK == 0 and lat_len % WINDOW == 0
    assert lat_len % EXPERT_TOKENS == 0 and n_ctx_tok % EXPERT_TOKENS == 0
    assert n_lat_seq + 1 <= MODS_PAD_ROWS

    cos_t, sin_t = _rope_tables(lat_len)
    cvec = jnp.zeros((MODS_PAD_ROWS, D_MODEL), F32).at[:n_lat_seq].set(c).at[n_lat_seq].set(c_ctx)
    mods = _mods(cvec, w_mod[0], b_mod[0])
    mods_lat = mods[:n_lat_seq].reshape(n_lat_seq, 1, N_MOD * D_MODEL)
    mods_ctx = mods[n_lat_seq:n_lat_seq + 1].reshape(1, 1, N_MOD * D_MODEL)

    half_q = PEER_DQ // 2
    w_in_b = w_in[0].astype(BF16)
    w_out_b = w_out[0].astype(BF16)
    wg = jnp.stack([_gate_weights(rg_w_a[0, 0]), _gate_weights(rg_w_i[0, 0]),
                    _gate_weights(rg_w_a[0, 1]), _gate_weights(rg_w_i[0, 1])]).astype(BF16)
    bg = jnp.stack([rg_b_a[0, 0], rg_b_i[0, 0], rg_b_a[0, 1], rg_b_i[0, 1]])
    gmix = g_norm_mix[0].reshape(1, D_MODEL)
    gffn = g_norm_ffn[0].reshape(1, D_MODEL)
    gfin = g_final.reshape(1, D_MODEL)
    cb = conv_b[0].reshape(1, D_RNN)
    sink = attn_sink[0]
    wq = peer_w_query[0].reshape(D_MODEL, PEER_HEADS, 2, half_q)
    wq_t = jnp.transpose(wq, (2, 1, 3, 0)).reshape(2 * PEER_HEADS * half_q, D_MODEL).astype(BF16)
    eye_h = jnp.eye(PEER_HEADS, dtype=F32)
    k1_big = jnp.einsum("hkq,hg->khgq", peer_sub_keys[0, :, 0], eye_h).reshape(
        N_KEYS * PEER_HEADS, PEER_HEADS * half_q).astype(BF16)
    k2 = peer_sub_keys[0, :, 1].astype(BF16)
    u_bf = peer_u[0].astype(BF16)
    vt_bf = peer_v[0].T.astype(BF16)

    h0_ctx = jnp.zeros((n_ctx_seq, 2, D_RNN), F32)
    x1_c, h2t_c, k_l, v_l, rnn_l = _ctx_mixer(
        x_prompt, mods_ctx, sink, gmix, gffn, w_in_b, conv_w[0], cb, wg, bg, rg_lambda[0], w_out_b, h0_ctx)

    ck = cache_k[:, 0].reshape(n_lat_seq, n_past, KV_W)
    cv = cache_v[:, 0].reshape(n_lat_seq, n_past, KV_W)
    x1_s, h2t_s = _lat_mixer(
        x_sample, mods_lat, sink, gmix, gffn, w_in_b, conv_w[0], cb, wg, bg, rg_lambda[0], w_out_b,
        state_rnn[:, 0], ck, cv, cos_t, sin_t)

    ga2_off = (N_MOD - 1) * D_MODEL
    outs = []
    for x1, h2t, ga2_rows, per_row in (
            (x1_c, h2t_c, mods[n_lat_seq:n_lat_seq + 1, ga2_off:], n_ctx_tok),
            (x1_s, h2t_s, mods[:n_lat_seq, ga2_off:], lat_len)):
        n_tok = h2t.shape[1]
        a1, n1, b2, r2 = _routing(h2t, wq_t, k1_big, k2)
        ga2 = _ga2_tiles(ga2_rows, per_row, n_tok)
        outs.append(_experts(h2t, u_bf, vt_bf, a1, n1, b2, r2, x1.reshape(n_tok, D_MODEL), ga2, gfin))
    y_prompt = outs[0].reshape(n_ctx_seq, ctx_len, D_MODEL)
    y_sample = outs[1].reshape(n_lat_seq, lat_len, D_MODEL)

    new_k = k_l.reshape(n_ctx_seq, 1, ctx_len, N_KV_HEADS, HEAD_DIM)
    new_v = v_l.reshape(n_ctx_seq, 1, ctx_len, N_KV_HEADS, HEAD_DIM)
    new_rnn = rnn_l[:, None]
    return (y_prompt, y_sample, new_k, new_v, new_rnn)
```
